```python
import math
import jax, jax.numpy as jnp
from jax import lax
import numpy as np

D_MODEL = 4096
BATCH = 2
SEQ = 4096
DEPTH = 1
DEC_BATCH = 32
DEC_SEQ = 8
PAST_LEN = 8192
PAGE_SIZE = 128

MIX_WIDTH = D_MODEL
SB_WIDTH = MIX_WIDTH // 2
SB_HEAD_DIM = 128
SB_HEADS = SB_WIDTH // SB_HEAD_DIM
ML_WIDTH = MIX_WIDTH - SB_WIDTH
ML_HEADS = 4
ML_V_DIM = ML_WIDTH // ML_HEADS
ML_QK_DIM = ML_V_DIM // 2
ML_CHUNK = 64
Q_BLOCK = 128
D_FF = -((-8 * D_MODEL) // (3 * 256)) * 256
SB_BIAS_LO = -8.0
SB_BIAS_HI = -5.0
EPS = 1e-6
F32 = jnp.float32

kernel_name = 'hymba_stickbreaking_mlstm_step'


def rms_norm(x, w):
    xf = x.astype(F32)
    y = xf * lax.rsqrt(jnp.mean(xf * xf, axis=-1, keepdims=True) + EPS)
    return (y * w.astype(F32)).astype(x.dtype)


def project(xn, w_in, b_igate, b_fgate):
    B, S, _ = xn.shape
    sizes = [SB_WIDTH] * 3 + [ML_HEADS * ML_QK_DIM] * 2 + [ML_WIDTH] * 2 + [ML_HEADS] * 2
    cuts = [int(c) for c in np.cumsum(sizes)[:-1]]
    p = jnp.einsum('bsd,dn->bsn', xn, w_in)
    q_a, k_a, v_a, q_b, k_b, v_b, o_b, i_b, f_b = jnp.split(p, cuts, axis=-1)
    sb = (q_a.reshape(B, S, SB_HEADS, SB_HEAD_DIM),
          k_a.reshape(B, S, SB_HEADS, SB_HEAD_DIM),
          v_a.reshape(B, S, SB_HEADS, SB_HEAD_DIM))
    ig = i_b.astype(F32) + b_igate.astype(F32)
    lf = jax.nn.log_sigmoid(f_b.astype(F32) + b_fgate.astype(F32))
    ml = (q_b.reshape(B, S, ML_HEADS, ML_QK_DIM),
          k_b.reshape(B, S, ML_HEADS, ML_QK_DIM) * (ML_QK_DIM ** -0.5),
          v_b.reshape(B, S, ML_HEADS, ML_V_DIM),
          o_b, ig, lf)
    return sb, ml


def sb_chunk(q, k, v, b_sb, valid, acc):
    z = (jnp.einsum('bthd,bshd->bhts', q.astype(F32), k.astype(F32)) * (SB_HEAD_DIM ** -0.5)
         + b_sb.astype(F32)[None, :, None, None])
    log_beta = jax.nn.log_sigmoid(z)
    log_keep = jnp.where(valid, jax.nn.log_sigmoid(-z), 0.0)
    after = lax.cumsum(log_keep, axis=3, reverse=True) - log_keep + acc[..., None]
    w = jnp.where(valid, jnp.exp(log_beta + after), 0.0)
    o = jnp.einsum('bhts,bshd->bthd', w, v.astype(F32))
    return o, acc + jnp.sum(log_keep, axis=3)


def sb_prompt(q, k, v, b_sb):
    B, S = q.shape[:2]
    pos = jnp.arange(S)

    def block(i):
        q_blk = lax.dynamic_slice_in_dim(q, i * Q_BLOCK, Q_BLOCK, axis=1)
        t = i * Q_BLOCK + jnp.arange(Q_BLOCK)
        valid = pos[None, :] < t[:, None]
        o, _ = sb_chunk(q_blk, k, v, b_sb, valid, jnp.zeros((B, SB_HEADS, Q_BLOCK), F32))
        return o

    o = lax.map(block, jnp.arange(S // Q_BLOCK))
    return jnp.moveaxis(o, 0, 1).reshape(B, S, SB_HEADS, SB_HEAD_DIM)


def sb_sample(q, k_new, v_new, b_sb, cache_k, cache_v, layer, page_table):
    B, T = q.shape[:2]
    t = jnp.arange(T)
    o, acc = sb_chunk(q, k_new, v_new, b_sb, t[None, :] < t[:, None], jnp.zeros((B, SB_HEADS, T), F32))
    valid_page = jnp.ones((T, PAGE_SIZE), bool)

    def step(carry, pages):
        o, acc = carry
        kp = cache_k[layer, pages]
        vp = cache_v[layer, pages]
        op, acc = sb_chunk(q, kp, vp, b_sb, valid_page, acc)
        return (o + op, acc), None

    (o, _), _ = lax.scan(step, (o, acc), page_table.T[::-1])
    return o


def mlstm_chunkwise(q, k, v, ig, lf, c0, n0, m0):
    B, S = q.shape[:2]
    L = math.gcd(S, ML_CHUNK)
    nc = S // L

    def to_chunks(a):
        a = a.astype(F32).reshape((B, nc, L) + a.shape[2:])
        return jnp.swapaxes(jnp.moveaxis(a, 1, 0), 2, 3)

    causal = jnp.tril(jnp.ones((L, L), bool))

    def step(carry, xs):
        c, n, m = carry
        qc, kc, vc, ic, fc = xs
        b = jnp.cumsum(fc, axis=-1)
        d = jnp.where(causal, b[..., :, None] - b[..., None, :] + ic[..., None, :], -jnp.inf)
        inter = b + m[..., None]
        m_t = jnp.maximum(inter, jnp.max(d, axis=-1))
        w_inter = jnp.exp(inter - m_t)
        w_intra = jnp.exp(d - m_t[..., None]) * jnp.einsum('bhtk,bhsk->bhts', qc, kc)
        num = (w_inter[..., None] * jnp.einsum('bhvk,bhtk->bhtv', c, qc)
               + jnp.einsum('bhts,bhsv->bhtv', w_intra, vc))
        den = w_inter * jnp.einsum('bhk,bhtk->bht', n, qc) + jnp.sum(w_intra, axis=-1)
        h = num / jnp.maximum(jnp.abs(den), jnp.exp(-m_t))[..., None]
        m_new = m_t[..., -1]
        g_state = jnp.exp(b[..., -1] + m - m_new)
        g_rows = jnp.exp(b[..., -1:] - b + ic - m_new[..., None])
        c_new = g_state[..., None, None] * c + jnp.einsum('bhsv,bhsk->bhvk', g_rows[..., None] * vc, kc)
        n_new = g_state[..., None] * n + jnp.einsum('bhs,bhsk->bhk', g_rows, kc)
        return (c_new, n_new, m_new), h

    xs = (to_chunks(q), to_chunks(k), to_chunks(v), to_chunks(ig), to_chunks(lf))
    (c, n, m), h = lax.scan(step, (c0.astype(F32), n0.astype(F32), m0.astype(F32)), xs)
    h = jnp.moveaxis(jnp.swapaxes(h, 2, 3), 0, 1).reshape(B, S, ML_HEADS, ML_V_DIM)
    return h, c, n, m


def mix_out(x, o_a, h_b, o_b, ml_norm_w, w_out):
    B, S, _ = x.shape
    hn = h_b * lax.rsqrt(jnp.mean(h_b * h_b, axis=-1, keepdims=True) + EPS)
    hn = hn.reshape(B, S, ML_WIDTH) * ml_norm_w.astype(F32) * jax.nn.sigmoid(o_b.astype(F32))
    cat = jnp.concatenate([o_a.reshape(B, S, SB_WIDTH), hn], axis=-1).astype(x.dtype)
    return x + jnp.einsum('bsm,md->bsd', cat, w_out)


def swiglu_ffn(x, norm_w, w_gate, w_up, w_down):
    xn = rms_norm(x, norm_w)
    hid = jax.nn.silu(jnp.einsum('bsd,df->bsf', xn, w_gate)) * jnp.einsum('bsd,df->bsf', xn, w_up)
    return x + jnp.einsum('bsf,fd->bsd', hid, w_down)


def setup_inputs(seed: int = 0) -> dict:
    key = jax.random.key(seed)
    ks = jax.random.split(key, 24)
    n_pages = PAST_LEN // PAGE_SIZE
    n_phys = (DEC_BATCH * n_pages * 5) // 4
    n_in = 3 * SB_WIDTH + 2 * ML_HEADS * ML_QK_DIM + 2 * ML_WIDTH + 2 * ML_HEADS
    nrm = lambda k, shape, s: jax.random.normal(k, shape, F32) * s
    perm = jax.random.permutation(ks[0], n_phys)[: DEC_BATCH * n_pages]
    return {
        'x_prompt': nrm(ks[1], (BATCH, SEQ, D_MODEL), 1.0),
        'x_sample': nrm(ks[2], (DEC_BATCH, DEC_SEQ, D_MODEL), 1.0),
        'cache_k': nrm(ks[3], (DEPTH, n_phys, PAGE_SIZE, SB_HEADS, SB_HEAD_DIM), 1.0),
        'cache_v': nrm(ks[4], (DEPTH, n_phys, PAGE_SIZE, SB_HEADS, SB_HEAD_DIM), 1.0),
        'page_table': perm.reshape(DEC_BATCH, n_pages).astype(jnp.int32),
        'state_c': nrm(ks[5], (DEPTH, DEC_BATCH, ML_HEADS, ML_V_DIM, ML_QK_DIM), 0.5),
        'state_n': nrm(ks[6], (DEPTH, DEC_BATCH, ML_HEADS, ML_QK_DIM), 0.5),
        'state_m': nrm(ks[7], (DEPTH, DEC_BATCH, ML_HEADS), 1.0),
        'norm_mix_w': 1.0 + nrm(ks[8], (DEPTH, D_MODEL), 0.1),
        'w_in': nrm(ks[9], (DEPTH, D_MODEL, n_in), D_MODEL ** -0.5),
        'b_sb': jnp.linspace(SB_BIAS_LO, SB_BIAS_HI, SB_HEADS, dtype=F32)[None, :] + nrm(ks[19], (DEPTH, SB_HEADS), 0.1),
        'b_igate': nrm(ks[10], (DEPTH, ML_HEADS), 0.1),
        'b_fgate': jnp.linspace(3.0, 6.0, ML_HEADS, dtype=F32)[None, :] + nrm(ks[11], (DEPTH, ML_HEADS), 0.1),
        'ml_norm_w': 1.0 + nrm(ks[12], (DEPTH, ML_WIDTH), 0.1),
        'w_out': nrm(ks[13], (DEPTH, MIX_WIDTH, D_MODEL), MIX_WIDTH ** -0.5),
        'norm_ffn_w': 1.0 + nrm(ks[14], (DEPTH, D_MODEL), 0.1),
        'w_gate': nrm(ks[15], (DEPTH, D_MODEL, D_FF), D_MODEL ** -0.5),
        'w_up': nrm(ks[16], (DEPTH, D_MODEL, D_FF), D_MODEL ** -0.5),
        'w_down': nrm(ks[17], (DEPTH, D_FF, D_MODEL), D_FF ** -0.5),
        'final_norm_w': 1.0 + nrm(ks[18], (D_MODEL,), 0.1),
    }


def reference(x_prompt, x_sample, cache_k, cache_v, page_table, state_c, state_n, state_m,
              norm_mix_w, w_in, b_sb, b_igate, b_fgate, ml_norm_w, w_out, norm_ffn_w, w_gate, w_up,
              w_down, final_norm_w):
    xp, xs = x_prompt, x_sample
    kp_l, vp_l, cp_l, np_l, mp_l = [], [], [], [], []
    ks_l, vs_l, cs_l, ns_l, ms_l = [], [], [], [], []
    for l in range(DEPTH):
        (q_a, k_a, v_a), (q_b, k_b, v_b, o_b, ig, lf) = project(
            rms_norm(xp, norm_mix_w[l]), w_in[l], b_igate[l], b_fgate[l])
        o_a = sb_prompt(q_a, k_a, v_a, b_sb[l])
        bp = xp.shape[0]
        h_b, c, n, m = mlstm_chunkwise(
            q_b, k_b, v_b, ig, lf,
            jnp.zeros((bp, ML_HEADS, ML_V_DIM, ML_QK_DIM), F32),
            jnp.zeros((bp, ML_HEADS, ML_QK_DIM), F32),
            jnp.zeros((bp, ML_HEADS), F32))
        xp = mix_out(xp, o_a, h_b, o_b, ml_norm_w[l], w_out[l])
        xp = swiglu_ffn(xp, norm_ffn_w[l], w_gate[l], w_up[l], w_down[l])
        kp_l.append(k_a); vp_l.append(v_a); cp_l.append(c); np_l.append(n); mp_l.append(m)

        (q_a, k_a, v_a), (q_b, k_b, v_b, o_b, ig, lf) = project(
            rms_norm(xs, norm_mix_w[l]), w_in[l], b_igate[l], b_fgate[l])
        o_a = sb_sample(q_a, k_a, v_a, b_sb[l], cache_k, cache_v, l, page_table)
        h_b, c, n, m = mlstm_chunkwise(q_b, k_b, v_b, ig, lf, state_c[l], state_n[l], state_m[l])
        xs = mix_out(xs, o_a, h_b, o_b, ml_norm_w[l], w_out[l])
        xs = swiglu_ffn(xs, norm_ffn_w[l], w_gate[l], w_up[l], w_down[l])
        ks_l.append(k_a); vs_l.append(v_a); cs_l.append(c); ns_l.append(n); ms_l.append(m)

    y_prompt = rms_norm(xp, final_norm_w)
    y_sample = rms_norm(xs, final_norm_w)
    return (y_prompt, y_sample,
            jnp.stack(kp_l), jnp.stack(vp_l), jnp.stack(cp_l), jnp.stack(np_l), jnp.stack(mp_l),
            jnp.stack(ks_l), jnp.stack(vs_l), jnp.stack(cs_l), jnp.stack(ns_l), jnp.stack(ms_l))
```

```python
import functools
import math

import numpy as np
import jax
import jax.numpy as jnp
from jax import lax
from jax.experimental import pallas as pl
from jax.experimental.pallas import tpu as pltpu

F32 = jnp.float32
BF16 = jnp.bfloat16
EPS = 1e-6

HEAD_DIM = 128
SB_HEADS = 16
SB_WIDTH = SB_HEADS * HEAD_DIM
ML_HEADS = 4
ML_QK = 256
ML_V = 512
ML_WIDTH = ML_HEADS * ML_V
PAGE = 128
LANES = 128
SUBLANES = 8

VMEM_LIMIT = 56 * 1024 * 1024


def _params(sem):
    return pltpu.CompilerParams(dimension_semantics=sem, vmem_limit_bytes=VMEM_LIMIT)


def _log_sigmoid(x):
    return jnp.minimum(x, 0.0) - jnp.log(1.0 + jnp.exp(-jnp.abs(x)))


def _div_pow2(x, n):
    assert n & (n - 1) == 0
    return lax.shift_right_logical(x, int(math.log2(n)))


def _split_bf16(x):
    hi = x.astype(BF16)
    lo = (x - hi.astype(F32)).astype(BF16)
    return hi, lo


def _rmsnorm_kernel(x_ref, w_ref, o_ref):
    x = x_ref[...]
    ms = jnp.mean(x * x, axis=-1, keepdims=True)
    o_ref[...] = (x * lax.rsqrt(ms + EPS) * w_ref[...]).astype(o_ref.dtype)


def _rmsnorm(x, w, out_dtype, tm):
    m, d = x.shape
    return pl.pallas_call(
        _rmsnorm_kernel,
        grid=(m // tm,),
        in_specs=[pl.BlockSpec((tm, d), lambda i: (i, 0)),
                  pl.BlockSpec((1, d), lambda i: (0, 0))],
        out_specs=pl.BlockSpec((tm, d), lambda i: (i, 0)),
        out_shape=jax.ShapeDtypeStruct((m, d), out_dtype),
        compiler_params=_params(("arbitrary",)),
        name="rmsnorm",
    )(x, w.reshape(1, d))


def _mm_kernel(a_ref, w_ref, *o_refs):
    acc = jnp.dot(a_ref[...].astype(BF16), w_ref[...], preferred_element_type=F32)
    for o_ref in o_refs:
        o_ref[...] = acc.astype(o_ref.dtype)


def _matmul(a, w, col_off, n_cols, tm, tn, out_dtypes, name):
    m, k = a.shape
    assert col_off % tn == 0 and n_cols % tn == 0 and m % tm == 0
    off = col_off // tn
    outs = pl.pallas_call(
        _mm_kernel,
        grid=(m // tm, n_cols // tn),
        in_specs=[pl.BlockSpec((tm, k), lambda i, j: (i, 0)),
                  pl.BlockSpec((k, tn), lambda i, j: (0, j + off))],
        out_specs=[pl.BlockSpec((tm, tn), lambda i, j: (i, j)) for _ in out_dtypes],
        out_shape=[jax.ShapeDtypeStruct((m, n_cols), dt) for dt in out_dtypes],
        compiler_params=_params(("arbitrary", "arbitrary")),
        name=name,
    )(a, w)
    return outs


def _outproj_kernel(a1_ref, a2_ref, w1_ref, w2_ref, r_ref, o_ref):
    acc = jnp.dot(a1_ref[...].astype(BF16), w1_ref[...], preferred_element_type=F32)
    acc += jnp.dot(a2_ref[...].astype(BF16), w2_ref[...], preferred_element_type=F32)
    o_ref[...] = r_ref[...] + acc


def _outproj(a1, a2, w, res, tm, tn):
    m, k1 = a1.shape
    k2 = a2.shape[1]
    n = w.shape[1]
    assert k1 == k2
    return pl.pallas_call(
        _outproj_kernel,
        grid=(m // tm, n // tn),
        in_specs=[pl.BlockSpec((tm, k1), lambda i, j: (i, 0)),
                  pl.BlockSpec((tm, k2), lambda i, j: (i, 0)),
                  pl.BlockSpec((k1, tn), lambda i, j: (0, j)),
                  pl.BlockSpec((k2, tn), lambda i, j: (1, j)),
                  pl.BlockSpec((tm, tn), lambda i, j: (i, j))],
        out_specs=pl.BlockSpec((tm, tn), lambda i, j: (i, j)),
        out_shape=jax.ShapeDtypeStruct((m, n), F32),
        compiler_params=_params(("arbitrary", "arbitrary")),
        name="outproj",
    )(a1, a2, w, w, res)


def _gateup_kernel(a_ref, wg_ref, wu_ref, o_ref):
    a = a_ref[...]
    g = jnp.dot(a, wg_ref[...], preferred_element_type=F32)
    u = jnp.dot(a, wu_ref[...], preferred_element_type=F32)
    o_ref[...] = (g / (1.0 + jnp.exp(-g)) * u).astype(o_ref.dtype)


def _gateup(a, wg, wu, tm, tn):
    m, k = a.shape
    f = wg.shape[1]
    assert f % tn == 0
    return pl.pallas_call(
        _gateup_kernel,
        grid=(m // tm, f // tn),
        in_specs=[pl.BlockSpec((tm, k), lambda i, j: (i, 0)),
                  pl.BlockSpec((k, tn), lambda i, j: (0, j)),
                  pl.BlockSpec((k, tn), lambda i, j: (0, j))],
        out_specs=pl.BlockSpec((tm, tn), lambda i, j: (i, j)),
        out_shape=jax.ShapeDtypeStruct((m, f), BF16),
        compiler_params=_params(("arbitrary", "arbitrary")),
        name="ffn_gateup",
    )(a, wg, wu)


def _down_kernel(a_ref, w_ref, r_ref, o_ref):
    o_ref[...] = r_ref[...] + jnp.dot(a_ref[...], w_ref[...], preferred_element_type=F32)


def _down(a, w, res, tm, tn):
    m, k = a.shape
    n = w.shape[1]
    return pl.pallas_call(
        _down_kernel,
        grid=(m // tm, n // tn),
        in_specs=[pl.BlockSpec((tm, k), lambda i, j: (i, 0)),
                  pl.BlockSpec((k, tn), lambda i, j: (0, j)),
                  pl.BlockSpec((tm, tn), lambda i, j: (i, j))],
        out_specs=pl.BlockSpec((tm, tn), lambda i, j: (i, j)),
        out_shape=jax.ShapeDtypeStruct((m, n), F32),
        compiler_params=_params(("arbitrary", "arbitrary")),
        name="ffn_down",
    )(a, w, res)


def _sb_prompt_kernel(qi_tab, kj_tab, q_ref, k_ref, v_ref, b_ref, tri_ref, o_ref, acc_o, acc_l, *, tq, tk):
    p = pl.program_id(2)
    qi = qi_tab[p]
    kj = kj_tab[p]

    @pl.when(kj == qi)
    def _():
        acc_o[...] = jnp.zeros_like(acc_o)
        acc_l[...] = jnp.zeros_like(acc_l)

    q = q_ref[...]
    bias = b_ref[0]
    scale = HEAD_DIM ** -0.5
    qpos = qi * tq + lax.broadcasted_iota(jnp.int32, (tq, LANES), 0)
    kbase = kj * tk + lax.broadcasted_iota(jnp.int32, (tq, LANES), 1)
    tri = tri_ref[...]
    for c in reversed(range(tk // LANES)):
        kc = k_ref[c * LANES:(c + 1) * LANES, :]
        vc = v_ref[c * LANES:(c + 1) * LANES, :]
        z = lax.dot_general(q, kc, (((1,), (1,)), ((), ())), preferred_element_type=F32) * scale + bias
        valid = (kbase + c * LANES) < qpos
        log_beta = _log_sigmoid(z)
        log_keep = jnp.where(valid, log_beta - z, 0.0)
        hi, lo = _split_bf16(log_keep)
        ct = jnp.dot(jnp.concatenate([hi, lo], axis=1), tri, preferred_element_type=F32)
        after = ct[:, :LANES] + acc_l[...]
        w = jnp.where(valid, jnp.exp(log_beta + after), 0.0)
        acc_o[...] += jnp.dot(w.astype(BF16), vc, preferred_element_type=F32)
        acc_l[...] += ct[:, LANES:]

    @pl.when(kj == 0)
    def _():
        o_ref[...] = acc_o[...].astype(o_ref.dtype)


def _later_key_matrix():
    j = np.arange(LANES)[:, None]
    s = np.arange(LANES)[None, :]
    tri = np.concatenate([(j > s).astype(np.float32), np.ones((LANES, LANES), np.float32)], axis=1)
    return jnp.asarray(np.concatenate([tri, tri], axis=0), dtype=BF16)


def _sb_prompt(q, k, v, b_sb, batch, seq, tq, tk, out_dtype):
    assert tq == tk and seq % tq == 0
    nq = seq // tq
    pairs = [(qi, qi - r) for qi in range(nq) for r in range(qi + 1)]
    qi_tab = jnp.asarray(np.array([p[0] for p in pairs], np.int32))
    kj_tab = jnp.asarray(np.array([p[1] for p in pairs], np.int32))
    heads = q.shape[1] // HEAD_DIM
    b_rep = jnp.broadcast_to(b_sb.astype(F32)[:, None, None], (heads, 1, LANES))
    grid_spec = pltpu.PrefetchScalarGridSpec(
        num_scalar_prefetch=2,
        grid=(batch, heads, len(pairs)),
        in_specs=[pl.BlockSpec((tq, HEAD_DIM), lambda b, h, p, qt, kt: (b * nq + qt[p], h)),
                  pl.BlockSpec((tk, HEAD_DIM), lambda b, h, p, qt, kt: (b * nq + kt[p], h)),
                  pl.BlockSpec((tk, HEAD_DIM), lambda b, h, p, qt, kt: (b * nq + kt[p], h)),
                  pl.BlockSpec((1, 1, LANES), lambda b, h, p, qt, kt: (h, 0, 0)),
                  pl.BlockSpec((2 * LANES, 2 * LANES), lambda b, h, p, qt, kt: (0, 0))],
        out_specs=pl.BlockSpec((tq, HEAD_DIM), lambda b, h, p, qt, kt: (b * nq + qt[p], h)),
        scratch_shapes=[pltpu.VMEM((tq, HEAD_DIM), F32), pltpu.VMEM((tq, LANES), F32)],
    )
    return pl.pallas_call(
        functools.partial(_sb_prompt_kernel, tq=tq, tk=tk),
        grid_spec=grid_spec,
        out_shape=jax.ShapeDtypeStruct(q.shape, out_dtype),
        compiler_params=_params(("arbitrary", "arbitrary", "arbitrary")),
        name="sb_prompt",
    )(qi_tab, kj_tab, q, k, v, b_rep, _later_key_matrix())


def _sb_sample_kernel(pt_ref, q_ref, kn_ref, vn_ref, b_ref, tri_ref, *rest, pages_per_step, t_new):
    npg = pages_per_step
    kp_refs = rest[:npg]
    vp_refs = rest[npg:2 * npg]
    o_ref = rest[2 * npg]
    qbd, acc_o, acc_l = rest[2 * npg + 1:]
    g = pl.program_id(1)
    heads = SB_WIDTH // HEAD_DIM
    bias = b_ref[...]
    tri = tri_ref[...]

    def process(kp, vp, valid):
        z = lax.dot_general(kp, qbd[...], (((1,), (1,)), ((), ())), preferred_element_type=F32) + bias
        log_beta = _log_sigmoid(z)
        log_keep = log_beta - z
        if valid is not None:
            log_keep = jnp.where(valid, log_keep, 0.0)
        hi, lo = _split_bf16(log_keep)
        after = jnp.dot(tri, jnp.concatenate([hi, lo], axis=0), preferred_element_type=F32) + acc_l[...]
        w = jnp.exp(log_beta + after)
        if valid is not None:
            w = jnp.where(valid, w, 0.0)
        acc_o[...] += jnp.dot(w.T.astype(BF16), vp, preferred_element_type=F32)
        acc_l[...] += jnp.sum(log_keep, axis=0, keepdims=True)

    @pl.when(g == 0)
    def _():
        rows = heads * t_new
        qt = jnp.concatenate([q_ref[...]] * heads, axis=0)
        rh = _div_pow2(lax.broadcasted_iota(jnp.int32, (rows, SB_WIDTH), 0), t_new)
        ch = _div_pow2(lax.broadcasted_iota(jnp.int32, (rows, SB_WIDTH), 1), HEAD_DIM)
        qbd[...] = jnp.where(rh == ch, qt * (HEAD_DIM ** -0.5), 0.0).astype(BF16)
        acc_o[...] = jnp.zeros_like(acc_o)
        acc_l[...] = jnp.zeros_like(acc_l)
        pad = jnp.zeros((PAGE - t_new, SB_WIDTH), F32)
        kn = jnp.concatenate([kn_ref[...], pad], axis=0).astype(BF16)
        vn = jnp.concatenate([vn_ref[...], pad], axis=0).astype(BF16)
        s_idx = lax.broadcasted_iota(jnp.int32, (PAGE, rows), 0)
        t_idx = lax.broadcasted_iota(jnp.int32, (PAGE, rows), 1) & (t_new - 1)
        process(kn, vn, s_idx < t_idx)

    def page(ref):
        return jnp.concatenate([ref[pl.ds(h, PAGE, stride=heads), :] for h in range(heads)], axis=1).astype(BF16)

    for i in range(npg):
        process(page(kp_refs[i]), page(vp_refs[i]), None)

    @pl.when(g == pl.num_programs(1) - 1)
    def _():
        for h in range(heads):
            o_ref[:, h * HEAD_DIM:(h + 1) * HEAD_DIM] = acc_o[h * t_new:(h + 1) * t_new,
                                                              h * HEAD_DIM:(h + 1) * HEAD_DIM]


def _sb_sample(q, k_new, v_new, b_sb, cache_k, cache_v, page_table, t_new, pages_per_step):
    batch, n_pages = page_table.shape
    assert n_pages % pages_per_step == 0 and t_new == SUBLANES and SB_HEADS * t_new == LANES
    steps = n_pages // pages_per_step
    j = np.arange(PAGE)
    tm = (j[None, :] > j[:, None]).astype(np.float32)
    tri = jnp.asarray(np.concatenate([tm, tm], axis=1), dtype=BF16)
    bias = jnp.repeat(b_sb.astype(F32), t_new)[None, :]

    def page_spec(i):
        return pl.BlockSpec((None, PAGE * SB_HEADS, HEAD_DIM),
                            lambda b, g, pt: (pt[b, n_pages - 1 - (g * pages_per_step + i)], 0, 0))

    row_spec = pl.BlockSpec((t_new, SB_WIDTH), lambda b, g, pt: (b, 0))
    grid_spec = pltpu.PrefetchScalarGridSpec(
        num_scalar_prefetch=1,
        grid=(batch, steps),
        in_specs=[row_spec, row_spec, row_spec,
                  pl.BlockSpec((1, LANES), lambda b, g, pt: (0, 0)),
                  pl.BlockSpec((PAGE, 2 * PAGE), lambda b, g, pt: (0, 0))]
                 + [page_spec(i) for i in range(pages_per_step)] * 2,
        out_specs=row_spec,
        scratch_shapes=[pltpu.VMEM((LANES, SB_WIDTH), BF16),
                        pltpu.VMEM((LANES, SB_WIDTH), F32),
                        pltpu.VMEM((1, LANES), F32)],
    )
    return pl.pallas_call(
        functools.partial(_sb_sample_kernel, pages_per_step=pages_per_step, t_new=t_new),
        grid_spec=grid_spec,
        out_shape=jax.ShapeDtypeStruct(q.shape, F32),
        compiler_params=_params(("arbitrary", "arbitrary")),
        name="sb_sample",
    )(page_table, q, k_new, v_new, bias, tri,
      *([cache_k] * pages_per_step), *([cache_v] * pages_per_step))


def _mlstm_kernel(*refs, chunk, group, has_init):
    if has_init:
        pb_ref, g_ref, gb_ref, nw_ref, tril_ref, c0_ref, n0_ref, m0_ref = refs[:8]
        refs = refs[8:]
    else:
        pb_ref, g_ref, gb_ref, nw_ref, tril_ref = refs[:5]
        refs = refs[5:]
    hn_ref, c_out, n_out, m_out, c_s, n_s, m_s = refs
    L = chunk
    ci = pl.program_id(1)

    @pl.when(ci == 0)
    def _():
        if has_init:
            c_s[...] = c0_ref[...]
            n_s[...] = n0_ref[...]
            m_s[...] = m0_ref[...]
        else:
            c_s[...] = jnp.zeros_like(c_s)
            n_s[...] = jnp.zeros_like(n_s)
            m_s[...] = jnp.zeros_like(m_s)

    row = lax.broadcasted_iota(jnp.int32, (L, L), 0)
    col = lax.broadcasted_iota(jnp.int32, (L, L), 1)
    mask = col <= row
    g = g_ref[...] + gb_ref[...]
    lf = _log_sigmoid(g)
    if group:
        live = pl.program_id(0) & (L // group - 1)
        rlive = _div_pow2(lax.broadcasted_iota(jnp.int32, (L, 1), 0), group) == live
        mask = jnp.logical_and(mask, _div_pow2(col, group) == live)
        lf = jnp.where(rlive, lf, 0.0)
    hi, lo = _split_bf16(lf)
    bsum = jnp.dot(tril_ref[...], jnp.concatenate([hi, lo], axis=0), preferred_element_type=F32)
    g_t = g.T
    bsum_t = bsum.T

    for h in range(ML_HEADS):
        qf = pb_ref[:, h * ML_QK:(h + 1) * ML_QK]
        kf = pb_ref[:, ML_HEADS * ML_QK + h * ML_QK:ML_HEADS * ML_QK + (h + 1) * ML_QK] * (ML_QK ** -0.5)
        v0 = 2 * ML_HEADS * ML_QK
        vf = pb_ref[:, v0 + h * ML_V:v0 + (h + 1) * ML_V]
        ob = pb_ref[:, v0 + ML_WIDTH + h * ML_V:v0 + ML_WIDTH + (h + 1) * ML_V]
        q = qf.astype(BF16)
        k = kf.astype(BF16)
        b_col = bsum[:, ML_HEADS + h:ML_HEADS + h + 1]
        b_row = bsum_t[ML_HEADS + h:ML_HEADS + h + 1, :]
        i_col = g[:, h:h + 1]
        i_row = g_t[h:h + 1, :]
        m_prev = m_s[h:h + 1, 0:1]
        c_st = c_s[h]
        n_st = n_s[h]

        d = jnp.where(mask, b_col - b_row + i_row, -jnp.inf)
        inter = b_col + m_prev
        m_t = jnp.maximum(inter, jnp.max(d, axis=1, keepdims=True))
        w_inter = jnp.exp(inter - m_t)
        w_intra = jnp.exp(d - m_t) * lax.dot_general(q, k, (((1,), (1,)), ((), ())), preferred_element_type=F32)
        qc = lax.dot_general(q, c_st.astype(BF16), (((1,), (1,)), ((), ())), preferred_element_type=F32)
        num = w_inter * qc + jnp.dot(w_intra.astype(BF16), vf.astype(BF16), preferred_element_type=F32)
        den = w_inter * jnp.sum(qf * n_st, axis=1, keepdims=True) + jnp.sum(w_intra, axis=1, keepdims=True)
        hh = num / jnp.maximum(jnp.abs(den), jnp.exp(-m_t))

        b_last = b_col[L - 1:L, :]
        m_new = m_t[L - 1:L, :]
        g_state = jnp.exp(b_last + m_prev - m_new)
        g_rows = jnp.exp(b_last - b_col + i_col - m_new)
        if group:
            g_rows = jnp.where(rlive, g_rows, 0.0)
        gv_t = (g_rows * vf).T.astype(BF16)
        c_s[h] = g_state * c_st + jnp.dot(gv_t, k, preferred_element_type=F32)
        n_s[h] = g_state * n_st + jnp.sum(g_rows * kf, axis=0, keepdims=True)
        m_s[h:h + 1, :] = jnp.broadcast_to(m_new, (1, LANES))

        hn = hh * lax.rsqrt(jnp.mean(hh * hh, axis=1, keepdims=True) + EPS)
        hn = hn * nw_ref[:, h * ML_V:(h + 1) * ML_V] / (1.0 + jnp.exp(-ob))
        if group:
            hn_ref[:, h * ML_V:(h + 1) * ML_V] = _rows(hn, live * group, group).astype(hn_ref.dtype)
        else:
            hn_ref[:, h * ML_V:(h + 1) * ML_V] = hn.astype(hn_ref.dtype)

    @pl.when(ci == pl.num_programs(1) - 1)
    def _():
        c_out[...] = c_s[...]
        n_out[...] = n_s[...]
        m_out[...] = m_s[...]


def _rows(x, r0, n):
    nrows = x.shape[0]
    ridx = lax.broadcasted_iota(jnp.int32, (nrows, 1), 0)
    xm = jnp.where(jnp.logical_and(ridx >= r0, ridx < r0 + n), x, 0.0)
    out = xm[0:n]
    for s in range(1, nrows // n):
        out = out + xm[s * n:(s + 1) * n]
    return out


def _mlstm(pb, gates, gate_bias, norm_w, batch, seq, chunk, group, init, hn_dtype):
    rows, width = pb.shape
    L = chunk
    t = np.arange(L)
    tril = (t[None, :] <= t[:, None]).astype(np.float32)
    tril2 = jnp.asarray(np.concatenate([tril, tril], axis=1), dtype=BF16)
    if group:
        per = L // group
        nc = 1
        blk = lambda b, c: (b // per, 0)
        out_rows = group
        hn_blk = lambda b, c: (b, 0)
    else:
        nc = seq // L
        blk = lambda b, c: (b * nc + c, 0)
        out_rows = L
        hn_blk = blk
    const = lambda b, c: (0, 0)
    state4 = lambda b, c: (b, 0, 0, 0)
    state3 = lambda b, c: (b, 0, 0)
    in_specs = [pl.BlockSpec((L, width), blk),
                pl.BlockSpec((L, LANES), blk),
                pl.BlockSpec((1, LANES), const),
                pl.BlockSpec((1, ML_WIDTH), const),
                pl.BlockSpec((L, 2 * L), const)]
    args = [pb, gates, gate_bias, norm_w.reshape(1, ML_WIDTH), tril2]
    c_spec = pl.BlockSpec((None, ML_HEADS, ML_V, ML_QK), state4)
    n_spec = pl.BlockSpec((None, ML_HEADS, 1, ML_QK), state4)
    m_spec = pl.BlockSpec((None, ML_HEADS, LANES), state3)
    if init is not None:
        in_specs += [c_spec, n_spec, m_spec]
        args += list(init)
    return pl.pallas_call(
        functools.partial(_mlstm_kernel, chunk=L, group=group, has_init=init is not None),
        grid=(batch, nc),
        in_specs=in_specs,
        out_specs=[pl.BlockSpec((out_rows, ML_WIDTH), hn_blk), c_spec, n_spec, m_spec],
        out_shape=[jax.ShapeDtypeStruct((rows, ML_WIDTH), hn_dtype),
                   jax.ShapeDtypeStruct((batch, ML_HEADS, ML_V, ML_QK), F32),
                   jax.ShapeDtypeStruct((batch, ML_HEADS, 1, ML_QK), F32),
                   jax.ShapeDtypeStruct((batch, ML_HEADS, LANES), F32)],
        scratch_shapes=[pltpu.VMEM((ML_HEADS, ML_V, ML_QK), F32),
                        pltpu.VMEM((ML_HEADS, 1, ML_QK), F32),
                        pltpu.VMEM((ML_HEADS, LANES), F32)],
        compiler_params=_params(("arbitrary", "arbitrary")),
        name="mlstm",
    )(*args)


def _layer(x, is_prompt, batch, seq, wts, extra):
    (norm_mix_w, w_in, w_gates, b_sb, gate_bias, ml_norm_w, w_out, norm_ffn_w, w_gate, w_up, w_down) = wts
    rows = x.shape[0]
    tm = 1024 if is_prompt else rows
    act = BF16 if is_prompt else F32

    xn = _rmsnorm(x, norm_mix_w, BF16, min(tm, 256))
    (q_a,) = _matmul(xn, w_in, 0, SB_WIDTH, tm, 512, [act], "inproj_q")
    k_outs = _matmul(xn, w_in, SB_WIDTH, SB_WIDTH, tm, 512, [F32, BF16] if is_prompt else [F32], "inproj_k")
    v_outs = _matmul(xn, w_in, 2 * SB_WIDTH, SB_WIDTH, tm, 512, [F32, BF16] if is_prompt else [F32], "inproj_v")
    (pb,) = _matmul(xn, w_in, 3 * SB_WIDTH, 3 * ML_WIDTH, tm, 512, [F32], "inproj_ml")
    (gates,) = _matmul(xn, w_gates, 0, LANES, tm, LANES, [F32], "inproj_gates")
    k_a, v_a = k_outs[0], v_outs[0]

    if is_prompt:
        o_a = _sb_prompt(q_a, k_outs[1], v_outs[1], b_sb, batch, seq, 512, 512, BF16)
        hn, c, n, m = _mlstm(pb, gates, gate_bias, ml_norm_w, batch, seq, 128, 0, None, BF16)
    else:
        cache_k, cache_v, page_table, init = extra
        o_a = _sb_sample(q_a, k_a, v_a, b_sb, cache_k, cache_v, page_table, seq, 4)
        hn, c, n, m = _mlstm(pb, gates, gate_bias, ml_norm_w, batch, seq, 128, seq, init, F32)

    x1 = _outproj(o_a, hn, w_out, x, tm, 512)
    xn2 = _rmsnorm(x1, norm_ffn_w, BF16, min(tm, 256))
    hid = _gateup(xn2, w_gate, w_up, tm, 256)
    x2 = _down(hid, w_down, x1, min(tm, 512), 256)
    return x2, k_a, v_a, c, n, m


def kernel(x_prompt, x_sample, cache_k, cache_v, page_table, state_c, state_n, state_m, norm_mix_w, w_in, b_sb,
           b_igate, b_fgate, ml_norm_w, w_out, norm_ffn_w, w_gate, w_up, w_down, final_norm_w):
    depth = w_in.shape[0]
    assert depth == 1
    bp, sp, d = x_prompt.shape
    bs, ss, _ = x_sample.shape
    n_main = 3 * SB_WIDTH + 3 * ML_WIDTH
    l = 0
    w_in_b = w_in[l].astype(BF16)
    w_gates = jnp.pad(w_in_b[:, n_main:], ((0, 0), (0, LANES - 2 * ML_HEADS)))
    gate_bias = jnp.concatenate([b_igate[l].astype(F32), b_fgate[l].astype(F32),
                                 jnp.zeros((LANES - 2 * ML_HEADS,), F32)])[None, :]
    wts = (norm_mix_w[l], w_in_b, w_gates, b_sb[l], gate_bias, ml_norm_w[l], w_out[l].astype(BF16),
           norm_ffn_w[l], w_gate[l].astype(BF16), w_up[l].astype(BF16), w_down[l].astype(BF16))

    xp, kp, vp, cp, np_, mp = _layer(x_prompt.reshape(bp * sp, d), True, bp, sp, wts, None)

    n_phys = cache_k.shape[1]
    init = (state_c[l], state_n[l][:, :, None, :],
            jnp.broadcast_to(state_m[l][:, :, None], (bs, ML_HEADS, LANES)))
    extra = (cache_k[l].reshape(n_phys, PAGE * SB_HEADS, HEAD_DIM), cache_v[l].reshape(n_phys, PAGE * SB_HEADS, HEAD_DIM),
             page_table, init)
    xs, ks, vs, cs, ns, ms = _layer(x_sample.reshape(bs * ss, d), False, bs, ss, wts, extra)

    y_prompt = _rmsnorm(xp, final_norm_w, F32, 256).reshape(bp, sp, d)
    y_sample = _rmsnorm(xs, final_norm_w, F32, 256).reshape(bs, ss, d)
    return (y_prompt, y_sample,
            kp.reshape(1, bp, sp, SB_HEADS, HEAD_DIM), vp.reshape(1, bp, sp, SB_HEADS, HEAD_DIM),
            cp[None], np_[:, :, 0, :][None], mp[:, :, 0][None],
            ks.reshape(1, bs, ss, SB_HEADS, HEAD_DIM), vs.reshape(1, bs, ss, SB_HEADS, HEAD_DIM),
            cs[None], ns[:, :, 0, :][None], ms[:, :, 0][None])
```

```python
import functools
import math

import numpy as np
import jax
import jax.numpy as jnp
from jax import lax
from jax.experimental import pallas as pl
from jax.experimental.pallas import tpu as pltpu

F32 = jnp.float32
BF16 = jnp.bfloat16
EPS = 1e-6

HEAD_DIM = 128
SB_HEADS = 16
SB_WIDTH = SB_HEADS * HEAD_DIM
ML_HEADS = 4
ML_QK = 256
ML_V = 512
ML_WIDTH = ML_HEADS * ML_V
PAGE = 128
PITCH = 24
LANES = 128
SUBLANES = 8

VMEM_LIMIT = 56 * 1024 * 1024


def _params(sem):
    return pltpu.CompilerParams(dimension_semantics=sem, vmem_limit_bytes=VMEM_LIMIT)


def _log_sigmoid(x):
    return jnp.minimum(x, 0.0) - jnp.log(1.0 + jnp.exp(-jnp.abs(x)))


def _div_pow2(x, n):
    assert n & (n - 1) == 0
    return lax.shift_right_logical(x, int(math.log2(n)))


def _split_bf16(x):
    hi = x.astype(BF16)
    lo = (x - hi.astype(F32)).astype(BF16)
    return hi, lo


def _rmsnorm_kernel(x_ref, w_ref, o_ref):
    x = x_ref[...]
    ms = jnp.mean(x * x, axis=-1, keepdims=True)
    o_ref[...] = (x * lax.rsqrt(ms + EPS) * w_ref[...]).astype(o_ref.dtype)


def _rmsnorm(x, w, out_dtype, tm):
    m, d = x.shape
    return pl.pallas_call(
        _rmsnorm_kernel,
        grid=(m // tm,),
        in_specs=[pl.BlockSpec((tm, d), lambda i: (i, 0)),
                  pl.BlockSpec((1, d), lambda i: (0, 0))],
        out_specs=pl.BlockSpec((tm, d), lambda i: (i, 0)),
        out_shape=jax.ShapeDtypeStruct((m, d), out_dtype),
        compiler_params=_params(("arbitrary",)),
        name="rmsnorm",
    )(x, w.reshape(1, d))


def _mm_kernel(a_ref, w_ref, *o_refs):
    acc = jnp.dot(a_ref[...].astype(BF16), w_ref[...], preferred_element_type=F32)
    for o_ref in o_refs:
        o_ref[...] = acc.astype(o_ref.dtype)


def _matmul(a, w, col_off, n_cols, tm, tn, out_dtypes, name):
    m, k = a.shape
    assert col_off % tn == 0 and n_cols % tn == 0 and m % tm == 0
    off = col_off // tn
    outs = pl.pallas_call(
        _mm_kernel,
        grid=(m // tm, n_cols // tn),
        in_specs=[pl.BlockSpec((tm, k), lambda i, j: (i, 0)),
                  pl.BlockSpec((k, tn), lambda i, j: (0, j + off))],
        out_specs=[pl.BlockSpec((tm, tn), lambda i, j: (i, j)) for _ in out_dtypes],
        out_shape=[jax.ShapeDtypeStruct((m, n_cols), dt) for dt in out_dtypes],
        compiler_params=_params(("arbitrary", "arbitrary")),
        name=name,
    )(a, w)
    return outs


def _mm_t_kernel(a_ref, w_ref, *o_refs, scale, tb):
    acc = jnp.dot(a_ref[...], w_ref[...], preferred_element_type=F32)
    for o_ref in o_refs[:-1]:
        o_ref[...] = acc.astype(o_ref.dtype)
    t_ref = o_refs[-1]
    tm, tn = acc.shape
    for blk in range(tm // tb):
        for hh in range(tn // HEAD_DIM):
            piece = acc[blk * tb:(blk + 1) * tb, hh * HEAD_DIM:(hh + 1) * HEAD_DIM] * scale
            t_ref[blk, hh] = piece.T.astype(t_ref.dtype)


def _matmul_t(a, w, col_off, n_cols, tm, tn, out_dtypes, scale, tb, name):
    m, k = a.shape
    assert col_off % tn == 0 and n_cols % tn == 0 and m % tm == 0 and tm % tb == 0 and tn % HEAD_DIM == 0
    off = col_off // tn
    heads = n_cols // HEAD_DIM
    return pl.pallas_call(
        functools.partial(_mm_t_kernel, scale=scale, tb=tb),
        grid=(m // tm, n_cols // tn),
        in_specs=[pl.BlockSpec((tm, k), lambda i, j: (i, 0)),
                  pl.BlockSpec((k, tn), lambda i, j: (0, j + off))],
        out_specs=[pl.BlockSpec((tm, tn), lambda i, j: (i, j)) for _ in out_dtypes]
                  + [pl.BlockSpec((tm // tb, tn // HEAD_DIM, HEAD_DIM, tb), lambda i, j: (i, j, 0, 0))],
        out_shape=[jax.ShapeDtypeStruct((m, n_cols), dt) for dt in out_dtypes]
                  + [jax.ShapeDtypeStruct((m // tb, heads, HEAD_DIM, tb), BF16)],
        compiler_params=_params(("arbitrary", "arbitrary")),
        name=name,
    )(a, w)


def _outproj_kernel(a1_ref, a2_ref, w1_ref, w2_ref, r_ref, o_ref):
    acc = jnp.dot(a1_ref[...].astype(BF16), w1_ref[...], preferred_element_type=F32)
    acc += jnp.dot(a2_ref[...].astype(BF16), w2_ref[...], preferred_element_type=F32)
    o_ref[...] = r_ref[...] + acc


def _outproj(a1, a2, w, res, tm, tn):
    m, k1 = a1.shape
    k2 = a2.shape[1]
    n = w.shape[1]
    assert k1 == k2
    return pl.pallas_call(
        _outproj_kernel,
        grid=(m // tm, n // tn),
        in_specs=[pl.BlockSpec((tm, k1), lambda i, j: (i, 0)),
                  pl.BlockSpec((tm, k2), lambda i, j: (i, 0)),
                  pl.BlockSpec((k1, tn), lambda i, j: (0, j)),
                  pl.BlockSpec((k2, tn), lambda i, j: (1, j)),
                  pl.BlockSpec((tm, tn), lambda i, j: (i, j))],
        out_specs=pl.BlockSpec((tm, tn), lambda i, j: (i, j)),
        out_shape=jax.ShapeDtypeStruct((m, n), F32),
        compiler_params=_params(("arbitrary", "arbitrary")),
        name="outproj",
    )(a1, a2, w, w, res)


def _gateup_kernel(a_ref, wg_ref, wu_ref, o_ref):
    a = a_ref[...]
    g = jnp.dot(a, wg_ref[...], preferred_element_type=F32)
    u = jnp.dot(a, wu_ref[...], preferred_element_type=F32)
    o_ref[...] = (g / (1.0 + jnp.exp(-g)) * u).astype(o_ref.dtype)


def _gateup(a, wg, wu, tm, tn):
    m, k = a.shape
    f = wg.shape[1]
    assert f % tn == 0
    return pl.pallas_call(
        _gateup_kernel,
        grid=(m // tm, f // tn),
        in_specs=[pl.BlockSpec((tm, k), lambda i, j: (i, 0)),
                  pl.BlockSpec((k, tn), lambda i, j: (0, j)),
                  pl.BlockSpec((k, tn), lambda i, j: (0, j))],
        out_specs=pl.BlockSpec((tm, tn), lambda i, j: (i, j)),
        out_shape=jax.ShapeDtypeStruct((m, f), BF16),
        compiler_params=_params(("arbitrary", "arbitrary")),
        name="ffn_gateup",
    )(a, wg, wu)


def _down_kernel(a_ref, w_ref, r_ref, o_ref):
    o_ref[...] = r_ref[...] + jnp.dot(a_ref[...], w_ref[...], preferred_element_type=F32)


def _down(a, w, res, tm, tn):
    m, k = a.shape
    n = w.shape[1]
    return pl.pallas_call(
        _down_kernel,
        grid=(m // tm, n // tn),
        in_specs=[pl.BlockSpec((tm, k), lambda i, j: (i, 0)),
                  pl.BlockSpec((k, tn), lambda i, j: (0, j)),
                  pl.BlockSpec((tm, tn), lambda i, j: (i, j))],
        out_specs=pl.BlockSpec((tm, tn), lambda i, j: (i, j)),
        out_shape=jax.ShapeDtypeStruct((m, n), F32),
        compiler_params=_params(("arbitrary", "arbitrary")),
        name="ffn_down",
    )(a, w, res)


KV_BLK = 256
TOT_ROWS = 16
LOG2E = 1.4426950408889634


def _neg_abs(x):
    bits = pltpu.bitcast(x, jnp.uint32) | jnp.uint32(0x80000000)
    return pltpu.bitcast(bits, F32)


def _sb_prompt_kernel(qt_ref, k_ref, vt_ref, b_ref, tri_ref, o_ref, ot_acc, *, tq, hpg):
    qi = pl.program_id(2)
    nqb = tq // KV_BLK
    qts = [jnp.concatenate([qt_ref[i, h] for i in range(nqb)], axis=1) for h in range(hpg)]
    tri = tri_ref[...]
    ot_acc[...] = jnp.zeros_like(ot_acc)

    def blocks(base, carry, masked):
        js = [base + u for u in reversed(range(nqb))]
        if masked:
            qpos = qi * tq + lax.broadcasted_iota(jnp.int32, (KV_BLK, tq), 1)
            valids = [(j * KV_BLK + lax.broadcasted_iota(jnp.int32, (KV_BLK, tq), 0)) < qpos for j in js]
        log_betas, log_keeps = [], []
        for h in range(hpg):
            for n, j in enumerate(js):
                kb = k_ref[pl.ds(pl.multiple_of(j * KV_BLK, KV_BLK), KV_BLK), h * HEAD_DIM:(h + 1) * HEAD_DIM]
                z = jnp.dot(kb, qts[h], preferred_element_type=F32) + b_ref[h]
                l = jnp.log(1.0 + jnp.exp2(_neg_abs(z))) * LOG2E
                log_beta = jnp.minimum(z, 0.0) - l
                log_keep = log_beta - z
                if masked:
                    log_keep = jnp.where(valids[n], log_keep, 0.0)
                log_betas.append(log_beta)
                log_keeps.append(log_keep.astype(BF16))
        cums = [jnp.dot(tri, lk, preferred_element_type=F32) for lk in log_keeps]
        new_carry = []
        for h in range(hpg):
            c = carry[h]
            ws = []
            for n in range(nqb):
                i = h * nqb + n
                w = jnp.exp2(log_betas[i] + cums[i][:KV_BLK] + c)
                if masked:
                    w = jnp.where(valids[n], w, 0.0)
                ws.append(w.astype(BF16))
                c = c + cums[i][KV_BLK:KV_BLK + 1]
            new_carry.append(c)
            vt = jnp.concatenate([vt_ref[j, h] for j in js], axis=1)
            ot_acc[h] += jnp.dot(vt, jnp.concatenate(ws, axis=0), preferred_element_type=F32)
        return tuple(new_carry)

    zero = tuple(jnp.zeros((1, tq), F32) for _ in range(hpg))
    carry = blocks(qi * nqb, zero, True)
    lax.fori_loop(0, qi, lambda i, c: blocks((qi - 1 - i) * nqb, c, False), carry)
    for h in range(hpg):
        o_ref[:, h * HEAD_DIM:(h + 1) * HEAD_DIM] = ot_acc[h].T.astype(o_ref.dtype)


def _later_key_matrix():
    s = np.arange(KV_BLK)[:, None]
    j = np.arange(KV_BLK)[None, :]
    tri = np.concatenate([(j > s).astype(np.float32), np.ones((TOT_ROWS, KV_BLK), np.float32)], axis=0)
    return jnp.asarray(tri, dtype=BF16)


def _sb_prompt(qt4, k, vt4, b_sb, batch, seq, tq, hpg, out_dtype):
    heads = k.shape[1] // HEAD_DIM
    assert seq % tq == 0 and tq % KV_BLK == 0 and heads % hpg == 0
    nq = seq // tq
    nqb = tq // KV_BLK
    nkb = seq // KV_BLK
    b_rep = jnp.broadcast_to((b_sb.astype(F32) * LOG2E)[:, None, None], (heads, 1, tq))
    return pl.pallas_call(
        functools.partial(_sb_prompt_kernel, tq=tq, hpg=hpg),
        grid=(batch, heads // hpg, nq),
        in_specs=[pl.BlockSpec((nqb, hpg, HEAD_DIM, KV_BLK), lambda b, g, qi: (b * nq + qi, g, 0, 0)),
                  pl.BlockSpec((seq, hpg * HEAD_DIM), lambda b, g, qi: (b, g)),
                  pl.BlockSpec((nkb, hpg, HEAD_DIM, KV_BLK), lambda b, g, qi: (b, g, 0, 0)),
                  pl.BlockSpec((hpg, 1, tq), lambda b, g, qi: (g, 0, 0)),
                  pl.BlockSpec((KV_BLK + TOT_ROWS, KV_BLK), lambda b, g, qi: (0, 0))],
        out_specs=pl.BlockSpec((tq, hpg * HEAD_DIM), lambda b, g, qi: (b * nq + qi, g)),
        out_shape=jax.ShapeDtypeStruct(k.shape, out_dtype),
        scratch_shapes=[pltpu.VMEM((hpg, HEAD_DIM, tq), F32)],
        compiler_params=_params(("arbitrary", "arbitrary", "arbitrary")),
        name="sb_prompt",
    )(qt4, k, vt4, b_rep, _later_key_matrix())


def _sb_sample_kernel(pt_ref, q_ref, kn_ref, vn_ref, b_ref, tri_ref, *rest, pages_per_step, t_new):
    npg = pages_per_step
    kp_refs = rest[:npg]
    vp_refs = rest[npg:2 * npg]
    o_ref = rest[2 * npg]
    qbd, acc_o, acc_l = rest[2 * npg + 1:2 * npg + 4]
    pads = rest[2 * npg + 4:]
    g = pl.program_id(1)
    heads = SB_WIDTH // HEAD_DIM
    bias = b_ref[...]
    tri = tri_ref[...]

    def process(kp, vp, valid):
        z = lax.dot_general(kp, qbd[...], (((1,), (1,)), ((), ())), preferred_element_type=F32) + bias
        log_beta = _log_sigmoid(z)
        log_keep = log_beta - z
        if valid is not None:
            log_keep = jnp.where(valid, log_keep, 0.0)
        hi, lo = _split_bf16(log_keep)
        after = jnp.dot(tri, jnp.concatenate([hi, lo], axis=0), preferred_element_type=F32) + acc_l[...]
        w = jnp.exp(log_beta + after)
        if valid is not None:
            w = jnp.where(valid, w, 0.0)
        acc_o[...] += jnp.dot(w.T.astype(BF16), vp, preferred_element_type=F32)
        acc_l[...] += jnp.sum(log_keep, axis=0, keepdims=True)

    @pl.when(g == 0)
    def _():
        rows = heads * t_new
        qt = jnp.concatenate([q_ref[...]] * heads, axis=0)
        rh = _div_pow2(lax.broadcasted_iota(jnp.int32, (rows, SB_WIDTH), 0), t_new)
        ch = _div_pow2(lax.broadcasted_iota(jnp.int32, (rows, SB_WIDTH), 1), HEAD_DIM)
        qbd[...] = jnp.where(rh == ch, qt * (HEAD_DIM ** -0.5), 0.0).astype(BF16)
        acc_o[...] = jnp.zeros_like(acc_o)
        acc_l[...] = jnp.zeros_like(acc_l)
        pad = jnp.zeros((PAGE - t_new, SB_WIDTH), F32)
        kn = jnp.concatenate([kn_ref[...], pad], axis=0).astype(BF16)
        vn = jnp.concatenate([vn_ref[...], pad], axis=0).astype(BF16)
        s_idx = lax.broadcasted_iota(jnp.int32, (PAGE, rows), 0)
        t_idx = lax.broadcasted_iota(jnp.int32, (PAGE, rows), 1) & (t_new - 1)
        process(kn, vn, s_idx < t_idx)

    def page(ref, pad):
        for s in range(PAGE):
            pad[PITCH * s:PITCH * s + heads, :] = ref[heads * s:heads * (s + 1), :]
        return jnp.concatenate([pad[pl.ds(h, PAGE, stride=PITCH), :] for h in range(heads)], axis=1).astype(BF16)

    for i in range(npg):
        process(page(kp_refs[i], pads[2 * i]), page(vp_refs[i], pads[2 * i + 1]), None)

    @pl.when(g == pl.num_programs(1) - 1)
    def _():
        for h in range(heads):
            o_ref[:, h * HEAD_DIM:(h + 1) * HEAD_DIM] = acc_o[h * t_new:(h + 1) * t_new,
                                                              h * HEAD_DIM:(h + 1) * HEAD_DIM]


def _sb_sample(q, k_new, v_new, b_sb, cache_k, cache_v, page_table, t_new, pages_per_step):
    batch, n_pages = page_table.shape
    assert n_pages % pages_per_step == 0 and t_new == SUBLANES and SB_HEADS * t_new == LANES
    steps = n_pages // pages_per_step
    j = np.arange(PAGE)
    tm = (j[None, :] > j[:, None]).astype(np.float32)
    tri = jnp.asarray(np.concatenate([tm, tm], axis=1), dtype=BF16)
    bias = jnp.repeat(b_sb.astype(F32), t_new)[None, :]

    def page_spec(i):
        return pl.BlockSpec((None, PAGE * SB_HEADS, HEAD_DIM),
                            lambda b, g, pt: (pt[b, n_pages - 1 - (g * pages_per_step + i)], 0, 0))

    row_spec = pl.BlockSpec((t_new, SB_WIDTH), lambda b, g, pt: (b, 0))
    grid_spec = pltpu.PrefetchScalarGridSpec(
        num_scalar_prefetch=1,
        grid=(batch, steps),
        in_specs=[row_spec, row_spec, row_spec,
                  pl.BlockSpec((1, LANES), lambda b, g, pt: (0, 0)),
                  pl.BlockSpec((PAGE, 2 * PAGE), lambda b, g, pt: (0, 0))]
                 + [page_spec(i) for i in range(pages_per_step)] * 2,
        out_specs=row_spec,
        scratch_shapes=[pltpu.VMEM((LANES, SB_WIDTH), BF16),
                        pltpu.VMEM((LANES, SB_WIDTH), F32),
                        pltpu.VMEM((1, LANES), F32)]
                       + [pltpu.VMEM((PAGE * PITCH, HEAD_DIM), F32) for _ in range(2 * pages_per_step)],
    )
    return pl.pallas_call(
        functools.partial(_sb_sample_kernel, pages_per_step=pages_per_step, t_new=t_new),
        grid_spec=grid_spec,
        out_shape=jax.ShapeDtypeStruct(q.shape, F32),
        compiler_params=_params(("arbitrary", "arbitrary")),
        name="sb_sample",
    )(page_table, q, k_new, v_new, bias, tri,
      *([cache_k] * pages_per_step), *([cache_v] * pages_per_step))


def _mlstm_kernel(*refs, chunk, group, has_init):
    if has_init:
        pb_ref, g_ref, gb_ref, nw_ref, tril_ref, c0_ref, n0_ref, m0_ref = refs[:8]
        refs = refs[8:]
    else:
        pb_ref, g_ref, gb_ref, nw_ref, tril_ref = refs[:5]
        refs = refs[5:]
    hn_ref, c_out, n_out, m_out, c_s, n_s, m_s = refs
    L = chunk
    ci = pl.program_id(1)

    @pl.when(ci == 0)
    def _():
        if has_init:
            c_s[...] = c0_ref[...]
            n_s[...] = n0_ref[...]
            m_s[...] = m0_ref[...]
        else:
            c_s[...] = jnp.zeros_like(c_s)
            n_s[...] = jnp.zeros_like(n_s)
            m_s[...] = jnp.zeros_like(m_s)

    row = lax.broadcasted_iota(jnp.int32, (L, L), 0)
    col = lax.broadcasted_iota(jnp.int32, (L, L), 1)
    mask = col <= row
    g = g_ref[...] + gb_ref[...]
    lf = _log_sigmoid(g)
    if group:
        live = pl.program_id(0) & (L // group - 1)
        rlive = _div_pow2(lax.broadcasted_iota(jnp.int32, (L, 1), 0), group) == live
        mask = jnp.logical_and(mask, _div_pow2(col, group) == live)
        lf = jnp.where(rlive, lf, 0.0)
    hi, lo = _split_bf16(lf)
    bsum = jnp.dot(tril_ref[...], jnp.concatenate([hi, lo], axis=0), preferred_element_type=F32)
    g_t = g.T
    bsum_t = bsum.T

    for h in range(ML_HEADS):
        qf = pb_ref[:, h * ML_QK:(h + 1) * ML_QK]
        kf = pb_ref[:, ML_HEADS * ML_QK + h * ML_QK:ML_HEADS * ML_QK + (h + 1) * ML_QK] * (ML_QK ** -0.5)
        v0 = 2 * ML_HEADS * ML_QK
        vf = pb_ref[:, v0 + h * ML_V:v0 + (h + 1) * ML_V]
        ob = pb_ref[:, v0 + ML_WIDTH + h * ML_V:v0 + ML_WIDTH + (h + 1) * ML_V]
        q = qf.astype(BF16)
        k = kf.astype(BF16)
        b_col = bsum[:, ML_HEADS + h:ML_HEADS + h + 1]
        b_row = bsum_t[ML_HEADS + h:ML_HEADS + h + 1, :]
        i_col = g[:, h:h + 1]
        i_row = g_t[h:h + 1, :]
        m_prev = m_s[h:h + 1, 0:1]
        c_st = c_s[h]
        n_st = n_s[h]

        d = jnp.where(mask, b_col - b_row + i_row, -jnp.inf)
        inter = b_col + m_prev
        m_t = jnp.maximum(inter, jnp.max(d, axis=1, keepdims=True))
        w_inter = jnp.exp(inter - m_t)
        w_intra = jnp.exp(d - m_t) * lax.dot_general(q, k, (((1,), (1,)), ((), ())), preferred_element_type=F32)
        qc = lax.dot_general(q, c_st.astype(BF16), (((1,), (1,)), ((), ())), preferred_element_type=F32)
        num = w_inter * qc + jnp.dot(w_intra.astype(BF16), vf.astype(BF16), preferred_element_type=F32)
        den = w_inter * jnp.sum(qf * n_st, axis=1, keepdims=True) + jnp.sum(w_intra, axis=1, keepdims=True)
        hh = num / jnp.maximum(jnp.abs(den), jnp.exp(-m_t))

        b_last = b_col[L - 1:L, :]
        m_new = m_t[L - 1:L, :]
        g_state = jnp.exp(b_last + m_prev - m_new)
        g_rows = jnp.exp(b_last - b_col + i_col - m_new)
        if group:
            g_rows = jnp.where(rlive, g_rows, 0.0)
        gv_t = (g_rows * vf).T.astype(BF16)
        c_s[h] = g_state * c_st + jnp.dot(gv_t, k, preferred_element_type=F32)
        n_s[h] = g_state * n_st + jnp.sum(g_rows * kf, axis=0, keepdims=True)
        m_s[h:h + 1, :] = jnp.broadcast_to(m_new, (1, LANES))

        hn = hh * lax.rsqrt(jnp.mean(hh * hh, axis=1, keepdims=True) + EPS)
        hn = hn * nw_ref[:, h * ML_V:(h + 1) * ML_V] / (1.0 + jnp.exp(-ob))
        if group:
            hn_ref[:, h * ML_V:(h + 1) * ML_V] = _rows(hn, live * group, group).astype(hn_ref.dtype)
        else:
            hn_ref[:, h * ML_V:(h + 1) * ML_V] = hn.astype(hn_ref.dtype)

    @pl.when(ci == pl.num_programs(1) - 1)
    def _():
        c_out[...] = c_s[...]
        n_out[...] = n_s[...]
        m_out[...] = m_s[...]


def _rows(x, r0, n):
    nrows = x.shape[0]
    ridx = lax.broadcasted_iota(jnp.int32, (nrows, 1), 0)
    xm = jnp.where(jnp.logical_and(ridx >= r0, ridx < r0 + n), x, 0.0)
    out = xm[0:n]
    for s in range(1, nrows // n):
        out = out + xm[s * n:(s + 1) * n]
    return out


def _mlstm(pb, gates, gate_bias, norm_w, batch, seq, chunk, group, init, hn_dtype):
    rows, width = pb.shape
    L = chunk
    t = np.arange(L)
    tril = (t[None, :] <= t[:, None]).astype(np.float32)
    tril2 = jnp.asarray(np.concatenate([tril, tril], axis=1), dtype=BF16)
    if group:
        per = L // group
        nc = 1
        blk = lambda b, c: (b // per, 0)
        out_rows = group
        hn_blk = lambda b, c: (b, 0)
    else:
        nc = seq // L
        blk = lambda b, c: (b * nc + c, 0)
        out_rows = L
        hn_blk = blk
    const = lambda b, c: (0, 0)
    state4 = lambda b, c: (b, 0, 0, 0)
    state3 = lambda b, c: (b, 0, 0)
    in_specs = [pl.BlockSpec((L, width), blk),
                pl.BlockSpec((L, LANES), blk),
                pl.BlockSpec((1, LANES), const),
                pl.BlockSpec((1, ML_WIDTH), const),
                pl.BlockSpec((L, 2 * L), const)]
    args = [pb, gates, gate_bias, norm_w.reshape(1, ML_WIDTH), tril2]
    c_spec = pl.BlockSpec((None, ML_HEADS, ML_V, ML_QK), state4)
    n_spec = pl.BlockSpec((None, ML_HEADS, 1, ML_QK), state4)
    m_spec = pl.BlockSpec((None, ML_HEADS, LANES), state3)
    if init is not None:
        in_specs += [c_spec, n_spec, m_spec]
        args += list(init)
    return pl.pallas_call(
        functools.partial(_mlstm_kernel, chunk=L, group=group, has_init=init is not None),
        grid=(batch, nc),
        in_specs=in_specs,
        out_specs=[pl.BlockSpec((out_rows, ML_WIDTH), hn_blk), c_spec, n_spec, m_spec],
        out_shape=[jax.ShapeDtypeStruct((rows, ML_WIDTH), hn_dtype),
                   jax.ShapeDtypeStruct((batch, ML_HEADS, ML_V, ML_QK), F32),
                   jax.ShapeDtypeStruct((batch, ML_HEADS, 1, ML_QK), F32),
                   jax.ShapeDtypeStruct((batch, ML_HEADS, LANES), F32)],
        scratch_shapes=[pltpu.VMEM((ML_HEADS, ML_V, ML_QK), F32),
                        pltpu.VMEM((ML_HEADS, 1, ML_QK), F32),
                        pltpu.VMEM((ML_HEADS, LANES), F32)],
        compiler_params=_params(("arbitrary", "arbitrary")),
        name="mlstm",
    )(*args)


def _layer(x, is_prompt, batch, seq, wts, extra):
    (norm_mix_w, w_in, w_gates, b_sb, gate_bias, ml_norm_w, w_out, norm_ffn_w, w_gate, w_up, w_down) = wts
    rows = x.shape[0]
    tm = 1024 if is_prompt else rows

    xn = _rmsnorm(x, norm_mix_w, BF16, min(tm, 256))
    (pb,) = _matmul(xn, w_in, 3 * SB_WIDTH, 3 * ML_WIDTH, tm, 512, [F32], "inproj_ml")
    (gates,) = _matmul(xn, w_gates, 0, LANES, tm, LANES, [F32], "inproj_gates")

    if is_prompt:
        (qt4,) = _matmul_t(xn, w_in, 0, SB_WIDTH, tm, 512, [], HEAD_DIM ** -0.5 * LOG2E, KV_BLK, "inproj_qt")
        k_a, k_b = _matmul(xn, w_in, SB_WIDTH, SB_WIDTH, tm, 512, [F32, BF16], "inproj_k")
        v_a, vt4 = _matmul_t(xn, w_in, 2 * SB_WIDTH, SB_WIDTH, tm, 512, [F32], 1.0, KV_BLK, "inproj_vt")
        o_a = _sb_prompt(qt4, k_b, vt4, b_sb, batch, seq, 512, 4, BF16)
        hn, c, n, m = _mlstm(pb, gates, gate_bias, ml_norm_w, batch, seq, 128, 0, None, BF16)
    else:
        (q_a,) = _matmul(xn, w_in, 0, SB_WIDTH, tm, 512, [F32], "inproj_q")
        (k_a,) = _matmul(xn, w_in, SB_WIDTH, SB_WIDTH, tm, 512, [F32], "inproj_k")
        (v_a,) = _matmul(xn, w_in, 2 * SB_WIDTH, SB_WIDTH, tm, 512, [F32], "inproj_v")
        cache_k, cache_v, page_table, init = extra
        o_a = _sb_sample(q_a, k_a, v_a, b_sb, cache_k, cache_v, page_table, seq, 4)
        hn, c, n, m = _mlstm(pb, gates, gate_bias, ml_norm_w, batch, seq, 128, seq, init, F32)

    x1 = _outproj(o_a, hn, w_out, x, tm, 512)
    xn2 = _rmsnorm(x1, norm_ffn_w, BF16, min(tm, 256))
    hid = _gateup(xn2, w_gate, w_up, tm, 256)
    x2 = _down(hid, w_down, x1, min(tm, 512), 256)
    return x2, k_a, v_a, c, n, m


def kernel(x_prompt, x_sample, cache_k, cache_v, page_table, state_c, state_n, state_m, norm_mix_w, w_in, b_sb,
           b_igate, b_fgate, ml_norm_w, w_out, norm_ffn_w, w_gate, w_up, w_down, final_norm_w):
    depth = w_in.shape[0]
    assert depth == 1
    bp, sp, d = x_prompt.shape
    bs, ss, _ = x_sample.shape
    n_main = 3 * SB_WIDTH + 3 * ML_WIDTH
    l = 0
    w_in_b = w_in[l].astype(BF16)
    w_gates = jnp.pad(w_in_b[:, n_main:], ((0, 0), (0, LANES - 2 * ML_HEADS)))
    gate_bias = jnp.concatenate([b_igate[l].astype(F32), b_fgate[l].astype(F32),
                                 jnp.zeros((LANES - 2 * ML_HEADS,), F32)])[None, :]
    wts = (norm_mix_w[l], w_in_b, w_gates, b_sb[l], gate_bias, ml_norm_w[l], w_out[l].astype(BF16),
           norm_ffn_w[l], w_gate[l].astype(BF16), w_up[l].astype(BF16), w_down[l].astype(BF16))

    xp, kp, vp, cp, np_, mp = _layer(x_prompt.reshape(bp * sp, d), True, bp, sp, wts, None)

    n_phys = cache_k.shape[1]
    init = (state_c[l], state_n[l][:, :, None, :],
            jnp.broadcast_to(state_m[l][:, :, None], (bs, ML_HEADS, LANES)))
    extra = (cache_k[l].reshape(n_phys, PAGE * SB_HEADS, HEAD_DIM), cache_v[l].reshape(n_phys, PAGE * SB_HEADS, HEAD_DIM),
             page_table, init)
    xs, ks, vs, cs, ns, ms = _layer(x_sample.reshape(bs * ss, d), False, bs, ss, wts, extra)

    y_prompt = _rmsnorm(xp, final_norm_w, F32, 256).reshape(bp, sp, d)
    y_sample = _rmsnorm(xs, final_norm_w, F32, 256).reshape(bs, ss, d)
    return (y_prompt, y_sample,
            kp.reshape(1, bp, sp, SB_HEADS, HEAD_DIM), vp.reshape(1, bp, sp, SB_HEADS, HEAD_DIM),
            cp[None], np_[:, :, 0, :][None], mp[:, :, 0][None],
            ks.reshape(1, bs, ss, SB_HEADS, HEAD_DIM), vs.reshape(1, bs, ss, SB_HEADS, HEAD_DIM),
            cs[None], ns[:, :, 0, :][None], ms[:, :, 0][None])
```

```python
import functools
import math

import numpy as np
import jax
import jax.numpy as jnp
from jax import lax
from jax.experimental import pallas as pl
from jax.experimental.pallas import tpu as pltpu

F32 = jnp.float32
BF16 = jnp.bfloat16
EPS = 1e-6

HEAD_DIM = 128
SB_HEADS = 16
SB_WIDTH = SB_HEADS * HEAD_DIM
ML_HEADS = 4
ML_QK = 256
ML_V = 512
ML_WIDTH = ML_HEADS * ML_V
PAGE = 128
PITCH = 24
LANES = 128
SUBLANES = 8

VMEM_LIMIT = 56 * 1024 * 1024


def _params(sem):
    return pltpu.CompilerParams(dimension_semantics=sem, vmem_limit_bytes=VMEM_LIMIT)


def _log_sigmoid(x):
    return jnp.minimum(x, 0.0) - jnp.log(1.0 + jnp.exp(-jnp.abs(x)))


def _div_pow2(x, n):
    assert n & (n - 1) == 0
    return lax.shift_right_logical(x, int(math.log2(n)))


def _split_bf16(x):
    hi = x.astype(BF16)
    lo = (x - hi.astype(F32)).astype(BF16)
    return hi, lo


def _rmsnorm_kernel(x_ref, w_ref, o_ref):
    x = x_ref[...]
    ms = jnp.mean(x * x, axis=-1, keepdims=True)
    o_ref[...] = (x * lax.rsqrt(ms + EPS) * w_ref[...]).astype(o_ref.dtype)


def _rmsnorm(x, w, out_dtype, tm):
    m, d = x.shape
    return pl.pallas_call(
        _rmsnorm_kernel,
        grid=(m // tm,),
        in_specs=[pl.BlockSpec((tm, d), lambda i: (i, 0)),
                  pl.BlockSpec((1, d), lambda i: (0, 0))],
        out_specs=pl.BlockSpec((tm, d), lambda i: (i, 0)),
        out_shape=jax.ShapeDtypeStruct((m, d), out_dtype),
        compiler_params=_params(("arbitrary",)),
        name="rmsnorm",
    )(x, w.reshape(1, d))


def _mm_kernel(a_ref, w_ref, *o_refs):
    acc = jnp.dot(a_ref[...].astype(BF16), w_ref[...], preferred_element_type=F32)
    for o_ref in o_refs:
        o_ref[...] = acc.astype(o_ref.dtype)


def _matmul(a, w, col_off, n_cols, tm, tn, out_dtypes, name):
    m, k = a.shape
    assert col_off % tn == 0 and n_cols % tn == 0 and m % tm == 0
    off = col_off // tn
    outs = pl.pallas_call(
        _mm_kernel,
        grid=(m // tm, n_cols // tn),
        in_specs=[pl.BlockSpec((tm, k), lambda i, j: (i, 0)),
                  pl.BlockSpec((k, tn), lambda i, j: (0, j + off))],
        out_specs=[pl.BlockSpec((tm, tn), lambda i, j: (i, j)) for _ in out_dtypes],
        out_shape=[jax.ShapeDtypeStruct((m, n_cols), dt) for dt in out_dtypes],
        compiler_params=_params(("arbitrary", "arbitrary")),
        name=name,
    )(a, w)
    return outs


def _mm_t_kernel(a_ref, w_ref, *o_refs, scale, tb):
    acc = jnp.dot(a_ref[...], w_ref[...], preferred_element_type=F32)
    for o_ref in o_refs[:-1]:
        o_ref[...] = acc.astype(o_ref.dtype)
    t_ref = o_refs[-1]
    tm, tn = acc.shape
    for blk in range(tm // tb):
        for hh in range(tn // HEAD_DIM):
            piece = acc[blk * tb:(blk + 1) * tb, hh * HEAD_DIM:(hh + 1) * HEAD_DIM] * scale
            t_ref[blk, hh] = piece.T.astype(t_ref.dtype)


def _matmul_t(a, w, col_off, n_cols, tm, tn, out_dtypes, scale, tb, name):
    m, k = a.shape
    assert col_off % tn == 0 and n_cols % tn == 0 and m % tm == 0 and tm % tb == 0 and tn % HEAD_DIM == 0
    off = col_off // tn
    heads = n_cols // HEAD_DIM
    return pl.pallas_call(
        functools.partial(_mm_t_kernel, scale=scale, tb=tb),
        grid=(m // tm, n_cols // tn),
        in_specs=[pl.BlockSpec((tm, k), lambda i, j: (i, 0)),
                  pl.BlockSpec((k, tn), lambda i, j: (0, j + off))],
        out_specs=[pl.BlockSpec((tm, tn), lambda i, j: (i, j)) for _ in out_dtypes]
                  + [pl.BlockSpec((tm // tb, tn // HEAD_DIM, HEAD_DIM, tb), lambda i, j: (i, j, 0, 0))],
        out_shape=[jax.ShapeDtypeStruct((m, n_cols), dt) for dt in out_dtypes]
                  + [jax.ShapeDtypeStruct((m // tb, heads, HEAD_DIM, tb), BF16)],
        compiler_params=_params(("arbitrary", "arbitrary")),
        name=name,
    )(a, w)


def _outproj_kernel(a1_ref, a2_ref, w1_ref, w2_ref, r_ref, o_ref):
    acc = jnp.dot(a1_ref[...].astype(BF16), w1_ref[...], preferred_element_type=F32)
    acc += jnp.dot(a2_ref[...].astype(BF16), w2_ref[...], preferred_element_type=F32)
    o_ref[...] = r_ref[...] + acc


def _outproj(a1, a2, w, res, tm, tn):
    m, k1 = a1.shape
    k2 = a2.shape[1]
    n = w.shape[1]
    assert k1 == k2
    return pl.pallas_call(
        _outproj_kernel,
        grid=(m // tm, n // tn),
        in_specs=[pl.BlockSpec((tm, k1), lambda i, j: (i, 0)),
                  pl.BlockSpec((tm, k2), lambda i, j: (i, 0)),
                  pl.BlockSpec((k1, tn), lambda i, j: (0, j)),
                  pl.BlockSpec((k2, tn), lambda i, j: (1, j)),
                  pl.BlockSpec((tm, tn), lambda i, j: (i, j))],
        out_specs=pl.BlockSpec((tm, tn), lambda i, j: (i, j)),
        out_shape=jax.ShapeDtypeStruct((m, n), F32),
        compiler_params=_params(("arbitrary", "arbitrary")),
        name="outproj",
    )(a1, a2, w, w, res)


def _gateup_kernel(a_ref, wg_ref, wu_ref, o_ref):
    a = a_ref[...]
    g = jnp.dot(a, wg_ref[...], preferred_element_type=F32)
    u = jnp.dot(a, wu_ref[...], preferred_element_type=F32)
    o_ref[...] = (g / (1.0 + jnp.exp(-g)) * u).astype(o_ref.dtype)


def _gateup(a, wg, wu, tm, tn):
    m, k = a.shape
    f = wg.shape[1]
    assert f % tn == 0
    return pl.pallas_call(
        _gateup_kernel,
        grid=(m // tm, f // tn),
        in_specs=[pl.BlockSpec((tm, k), lambda i, j: (i, 0)),
                  pl.BlockSpec((k, tn), lambda i, j: (0, j)),
                  pl.BlockSpec((k, tn), lambda i, j: (0, j))],
        out_specs=pl.BlockSpec((tm, tn), lambda i, j: (i, j)),
        out_shape=jax.ShapeDtypeStruct((m, f), BF16),
        compiler_params=_params(("arbitrary", "arbitrary")),
        name="ffn_gateup",
    )(a, wg, wu)


def _down_kernel(a_ref, w_ref, r_ref, o_ref):
    o_ref[...] = r_ref[...] + jnp.dot(a_ref[...], w_ref[...], preferred_element_type=F32)


def _down(a, w, res, tm, tn):
    m, k = a.shape
    n = w.shape[1]
    return pl.pallas_call(
        _down_kernel,
        grid=(m // tm, n // tn),
        in_specs=[pl.BlockSpec((tm, k), lambda i, j: (i, 0)),
                  pl.BlockSpec((k, tn), lambda i, j: (0, j)),
                  pl.BlockSpec((tm, tn), lambda i, j: (i, j))],
        out_specs=pl.BlockSpec((tm, tn), lambda i, j: (i, j)),
        out_shape=jax.ShapeDtypeStruct((m, n), F32),
        compiler_params=_params(("arbitrary", "arbitrary")),
        name="ffn_down",
    )(a, w, res)


KV_BLK = 256
TOT_ROWS = 16
LOG2E = 1.4426950408889634


def _neg_abs(x):
    bits = pltpu.bitcast(x, jnp.uint32) | jnp.uint32(0x80000000)
    return pltpu.bitcast(bits, F32)


def _sb_prompt_kernel(qt_ref, k_ref, vt_ref, b_ref, tri_ref, o_ref, ot_acc, *, tq, hpg):
    qi = pl.program_id(2)
    nqb = tq // KV_BLK
    qts = [jnp.concatenate([qt_ref[i, h] for i in range(nqb)], axis=1) for h in range(hpg)]
    tri = tri_ref[...]
    ot_acc[...] = jnp.zeros_like(ot_acc)

    def blocks(base, carry, masked):
        js = [base + u for u in reversed(range(nqb))]
        if masked:
            qpos = qi * tq + lax.broadcasted_iota(jnp.int32, (KV_BLK, tq), 1)
            valids = [(j * KV_BLK + lax.broadcasted_iota(jnp.int32, (KV_BLK, tq), 0)) < qpos for j in js]
        log_betas, log_keeps = [], []
        for h in range(hpg):
            for n, j in enumerate(js):
                kb = k_ref[pl.ds(pl.multiple_of(j * KV_BLK, KV_BLK), KV_BLK), h * HEAD_DIM:(h + 1) * HEAD_DIM]
                z = jnp.dot(kb, qts[h], preferred_element_type=F32) + b_ref[h]
                l = jnp.log(1.0 + jnp.exp2(_neg_abs(z))) * LOG2E
                log_beta = jnp.minimum(z, 0.0) - l
                log_keep = log_beta - z
                if masked:
                    log_keep = jnp.where(valids[n], log_keep, 0.0)
                log_betas.append(log_beta)
                log_keeps.append(log_keep.astype(BF16))
        cums = [jnp.dot(tri, lk, preferred_element_type=F32) for lk in log_keeps]
        new_carry = []
        for h in range(hpg):
            c = carry[h]
            ws = []
            for n in range(nqb):
                i = h * nqb + n
                w = jnp.exp2(log_betas[i] + cums[i][:KV_BLK] + c)
                if masked:
                    w = jnp.where(valids[n], w, 0.0)
                ws.append(w.astype(BF16))
                c = c + cums[i][KV_BLK:KV_BLK + 1]
            new_carry.append(c)
            vt = jnp.concatenate([vt_ref[j, h] for j in js], axis=1)
            ot_acc[h] += jnp.dot(vt, jnp.concatenate(ws, axis=0), preferred_element_type=F32)
        return tuple(new_carry)

    zero = tuple(jnp.zeros((1, tq), F32) for _ in range(hpg))
    carry = blocks(qi * nqb, zero, True)
    lax.fori_loop(0, qi, lambda i, c: blocks((qi - 1 - i) * nqb, c, False), carry)
    for h in range(hpg):
        o_ref[:, h * HEAD_DIM:(h + 1) * HEAD_DIM] = ot_acc[h].T.astype(o_ref.dtype)


def _later_key_matrix():
    s = np.arange(KV_BLK)[:, None]
    j = np.arange(KV_BLK)[None, :]
    tri = np.concatenate([(j > s).astype(np.float32), np.ones((TOT_ROWS, KV_BLK), np.float32)], axis=0)
    return jnp.asarray(tri, dtype=BF16)


def _sb_prompt(qt4, k, vt4, b_sb, batch, seq, tq, hpg, out_dtype):
    heads = k.shape[1] // HEAD_DIM
    assert seq % tq == 0 and tq % KV_BLK == 0 and heads % hpg == 0
    nq = seq // tq
    nqb = tq // KV_BLK
    nkb = seq // KV_BLK
    b_rep = jnp.broadcast_to((b_sb.astype(F32) * LOG2E)[:, None, None], (heads, 1, tq))
    return pl.pallas_call(
        functools.partial(_sb_prompt_kernel, tq=tq, hpg=hpg),
        grid=(batch, heads // hpg, nq),
        in_specs=[pl.BlockSpec((nqb, hpg, HEAD_DIM, KV_BLK), lambda b, g, qi: (b * nq + qi, g, 0, 0)),
                  pl.BlockSpec((seq, hpg * HEAD_DIM), lambda b, g, qi: (b, g)),
                  pl.BlockSpec((nkb, hpg, HEAD_DIM, KV_BLK), lambda b, g, qi: (b, g, 0, 0)),
                  pl.BlockSpec((hpg, 1, tq), lambda b, g, qi: (g, 0, 0)),
                  pl.BlockSpec((KV_BLK + TOT_ROWS, KV_BLK), lambda b, g, qi: (0, 0))],
        out_specs=pl.BlockSpec((tq, hpg * HEAD_DIM), lambda b, g, qi: (b * nq + qi, g)),
        out_shape=jax.ShapeDtypeStruct(k.shape, out_dtype),
        scratch_shapes=[pltpu.VMEM((hpg, HEAD_DIM, tq), F32)],
        compiler_params=_params(("arbitrary", "arbitrary", "arbitrary")),
        name="sb_prompt",
    )(qt4, k, vt4, b_rep, _later_key_matrix())


def _sb_sample_kernel(pt_ref, q_ref, kn_ref, vn_ref, b_ref, tri_ref, ck_hbm, cv_hbm, o_ref,
                      qbd, acc_o, acc_l, kbuf, vbuf, sems, *, pages_per_step, t_new, n_pages):
    npg = pages_per_step
    b = pl.program_id(0)
    g = pl.program_id(1)
    steps = n_pages // npg
    n = b * steps + g
    total = pl.num_programs(0) * steps
    heads = SB_WIDTH // HEAD_DIM
    bias = b_ref[...]
    tri = tri_ref[...]

    def page_copies(m):
        bm = m // steps
        gm = m - bm * steps
        first = (m % 2) * npg
        copies = []
        for i in range(npg):
            page = pt_ref[bm, n_pages - 1 - (gm * npg + i)]
            for src, buf, kv in ((ck_hbm, kbuf, 0), (cv_hbm, vbuf, 1)):
                copies.append(pltpu.make_async_copy(src.at[page], buf.at[first + i, :, pl.ds(0, heads), :],
                                                    sems.at[kv, first + i]))
        return copies

    @pl.when(n == 0)
    def _():
        for c in page_copies(n):
            c.start()

    @pl.when(n + 1 < total)
    def _():
        for c in page_copies(n + 1):
            c.start()

    def process(kps, vps, valid):
        kall = jnp.concatenate(kps, axis=0) if len(kps) > 1 else kps[0]
        vall = jnp.concatenate(vps, axis=0) if len(vps) > 1 else vps[0]
        z = lax.dot_general(kall, qbd[...], (((1,), (1,)), ((), ())), preferred_element_type=F32) + bias
        log_beta = _log_sigmoid(z)
        log_keep = log_beta - z
        if valid is not None:
            log_keep = jnp.where(valid, log_keep, 0.0)
        carry = acc_l[...]
        afters = []
        for i in range(len(kps)):
            lk = log_keep[i * PAGE:(i + 1) * PAGE]
            hi, lo = _split_bf16(lk)
            afters.append(jnp.dot(tri, jnp.concatenate([hi, lo], axis=0), preferred_element_type=F32) + carry)
            carry = carry + jnp.sum(lk, axis=0, keepdims=True)
        acc_l[...] = carry
        w = jnp.exp(log_beta + (jnp.concatenate(afters, axis=0) if len(afters) > 1 else afters[0]))
        if valid is not None:
            w = jnp.where(valid, w, 0.0)
        acc_o[...] += jnp.dot(w.T.astype(BF16), vall, preferred_element_type=F32)

    @pl.when(g == 0)
    def _():
        rows = heads * t_new
        qt = jnp.concatenate([q_ref[...]] * heads, axis=0)
        rh = _div_pow2(lax.broadcasted_iota(jnp.int32, (rows, SB_WIDTH), 0), t_new)
        ch = _div_pow2(lax.broadcasted_iota(jnp.int32, (rows, SB_WIDTH), 1), HEAD_DIM)
        qbd[...] = jnp.where(rh == ch, qt * (HEAD_DIM ** -0.5), 0.0).astype(BF16)
        acc_o[...] = jnp.zeros_like(acc_o)
        acc_l[...] = jnp.zeros_like(acc_l)
        pad = jnp.zeros((PAGE - t_new, SB_WIDTH), F32)
        kn = jnp.concatenate([kn_ref[...], pad], axis=0).astype(BF16)
        vn = jnp.concatenate([vn_ref[...], pad], axis=0).astype(BF16)
        s_idx = lax.broadcasted_iota(jnp.int32, (PAGE, rows), 0)
        t_idx = lax.broadcasted_iota(jnp.int32, (PAGE, rows), 1) & (t_new - 1)
        process([kn], [vn], s_idx < t_idx)

    def page(buf, idx):
        flat = buf.at[idx].reshape(PAGE * PITCH, HEAD_DIM)
        return jnp.concatenate([flat[pl.ds(h, PAGE, stride=PITCH), :] for h in range(heads)], axis=1).astype(BF16)

    for c in page_copies(n):
        c.wait()
    first = (n % 2) * npg
    process([page(kbuf, first + i) for i in range(npg)], [page(vbuf, first + i) for i in range(npg)], None)

    @pl.when(g == pl.num_programs(1) - 1)
    def _():
        for h in range(heads):
            o_ref[:, h * HEAD_DIM:(h + 1) * HEAD_DIM] = acc_o[h * t_new:(h + 1) * t_new,
                                                              h * HEAD_DIM:(h + 1) * HEAD_DIM]


def _sb_sample(q, k_new, v_new, b_sb, cache_k, cache_v, page_table, t_new, pages_per_step):
    batch, n_pages = page_table.shape
    assert n_pages % pages_per_step == 0 and t_new == SUBLANES and SB_HEADS * t_new == LANES
    steps = n_pages // pages_per_step
    j = np.arange(PAGE)
    tm = (j[None, :] > j[:, None]).astype(np.float32)
    tri = jnp.asarray(np.concatenate([tm, tm], axis=1), dtype=BF16)
    bias = jnp.repeat(b_sb.astype(F32), t_new)[None, :]
    row_spec = pl.BlockSpec((t_new, SB_WIDTH), lambda b, g, pt: (b, 0))
    n_bufs = 2 * pages_per_step
    grid_spec = pltpu.PrefetchScalarGridSpec(
        num_scalar_prefetch=1,
        grid=(batch, steps),
        in_specs=[row_spec, row_spec, row_spec,
                  pl.BlockSpec((1, LANES), lambda b, g, pt: (0, 0)),
                  pl.BlockSpec((PAGE, 2 * PAGE), lambda b, g, pt: (0, 0)),
                  pl.BlockSpec(memory_space=pl.ANY),
                  pl.BlockSpec(memory_space=pl.ANY)],
        out_specs=row_spec,
        scratch_shapes=[pltpu.VMEM((LANES, SB_WIDTH), BF16),
                        pltpu.VMEM((LANES, SB_WIDTH), F32),
                        pltpu.VMEM((1, LANES), F32),
                        pltpu.VMEM((n_bufs, PAGE, PITCH, HEAD_DIM), F32),
                        pltpu.VMEM((n_bufs, PAGE, PITCH, HEAD_DIM), F32),
                        pltpu.SemaphoreType.DMA((2, n_bufs))],
    )
    return pl.pallas_call(
        functools.partial(_sb_sample_kernel, pages_per_step=pages_per_step, t_new=t_new, n_pages=n_pages),
        grid_spec=grid_spec,
        out_shape=jax.ShapeDtypeStruct(q.shape, F32),
        compiler_params=_params(("arbitrary", "arbitrary")),
        name="sb_sample",
    )(page_table, q, k_new, v_new, bias, tri, cache_k, cache_v)


def _mlstm_kernel(*refs, chunk, group, has_init):
    if has_init:
        pb_ref, g_ref, gb_ref, nw_ref, tril_ref, c0_ref, n0_ref, m0_ref = refs[:8]
        refs = refs[8:]
    else:
        pb_ref, g_ref, gb_ref, nw_ref, tril_ref = refs[:5]
        refs = refs[5:]
    hn_ref, c_out, n_out, m_out, c_s, n_s, m_s = refs
    L = chunk
    ci = pl.program_id(1)

    @pl.when(ci == 0)
    def _():
        if has_init:
            c_s[...] = c0_ref[...]
            n_s[...] = n0_ref[...]
            m_s[...] = m0_ref[...]
        else:
            c_s[...] = jnp.zeros_like(c_s)
            n_s[...] = jnp.zeros_like(n_s)
            m_s[...] = jnp.zeros_like(m_s)

    row = lax.broadcasted_iota(jnp.int32, (L, L), 0)
    col = lax.broadcasted_iota(jnp.int32, (L, L), 1)
    mask = col <= row
    g = g_ref[...] + gb_ref[...]
    lf = _log_sigmoid(g)
    if group:
        live = pl.program_id(0) & (L // group - 1)
        rlive = _div_pow2(lax.broadcasted_iota(jnp.int32, (L, 1), 0), group) == live
        mask = jnp.logical_and(mask, _div_pow2(col, group) == live)
        lf = jnp.where(rlive, lf, 0.0)
    hi, lo = _split_bf16(lf)
    bsum = jnp.dot(tril_ref[...], jnp.concatenate([hi, lo], axis=0), preferred_element_type=F32)
    g_t = g.T
    bsum_t = bsum.T

    for h in range(ML_HEADS):
        qf = pb_ref[:, h * ML_QK:(h + 1) * ML_QK]
        kf = pb_ref[:, ML_HEADS * ML_QK + h * ML_QK:ML_HEADS * ML_QK + (h + 1) * ML_QK] * (ML_QK ** -0.5)
        v0 = 2 * ML_HEADS * ML_QK
        vf = pb_ref[:, v0 + h * ML_V:v0 + (h + 1) * ML_V]
        ob = pb_ref[:, v0 + ML_WIDTH + h * ML_V:v0 + ML_WIDTH + (h + 1) * ML_V]
        q = qf.astype(BF16)
        k = kf.astype(BF16)
        b_col = bsum[:, ML_HEADS + h:ML_HEADS + h + 1]
        b_row = bsum_t[ML_HEADS + h:ML_HEADS + h + 1, :]
        i_col = g[:, h:h + 1]
        i_row = g_t[h:h + 1, :]
        m_prev = m_s[h:h + 1, 0:1]
        c_st = c_s[h]
        n_st = n_s[h]

        d = jnp.where(mask, b_col - b_row + i_row, -jnp.inf)
        inter = b_col + m_prev
        m_t = jnp.maximum(inter, jnp.max(d, axis=1, keepdims=True))
        w_inter = jnp.exp(inter - m_t)
        w_intra = jnp.exp(d - m_t) * lax.dot_general(q, k, (((1,), (1,)), ((), ())), preferred_element_type=F32)
        qc = lax.dot_general(q, c_st.astype(BF16), (((1,), (1,)), ((), ())), preferred_element_type=F32)
        num = w_inter * qc + jnp.dot(w_intra.astype(BF16), vf.astype(BF16), preferred_element_type=F32)
        den = w_inter * jnp.sum(qf * n_st, axis=1, keepdims=True) + jnp.sum(w_intra, axis=1, keepdims=True)
        hh = num / jnp.maximum(jnp.abs(den), jnp.exp(-m_t))

        b_last = b_col[L - 1:L, :]
        m_new = m_t[L - 1:L, :]
        g_state = jnp.exp(b_last + m_prev - m_new)
        g_rows = jnp.exp(b_last - b_col + i_col - m_new)
        if group:
            g_rows = jnp.where(rlive, g_rows, 0.0)
        gv_t = (g_rows * vf).T.astype(BF16)
        c_s[h] = g_state * c_st + jnp.dot(gv_t, k, preferred_element_type=F32)
        n_s[h] = g_state * n_st + jnp.sum(g_rows * kf, axis=0, keepdims=True)
        m_s[h:h + 1, :] = jnp.broadcast_to(m_new, (1, LANES))

        hn = hh * lax.rsqrt(jnp.mean(hh * hh, axis=1, keepdims=True) + EPS)
        hn = hn * nw_ref[:, h * ML_V:(h + 1) * ML_V] / (1.0 + jnp.exp(-ob))
        if group:
            hn_ref[:, h * ML_V:(h + 1) * ML_V] = _rows(hn, live * group, group).astype(hn_ref.dtype)
        else:
            hn_ref[:, h * ML_V:(h + 1) * ML_V] = hn.astype(hn_ref.dtype)

    @pl.when(ci == pl.num_programs(1) - 1)
    def _():
        c_out[...] = c_s[...]
        n_out[...] = n_s[...]
        m_out[...] = m_s[...]


def _rows(x, r0, n):
    nrows = x.shape[0]
    ridx = lax.broadcasted_iota(jnp.int32, (nrows, 1), 0)
    xm = jnp.where(jnp.logical_and(ridx >= r0, ridx < r0 + n), x, 0.0)
    out = xm[0:n]
    for s in range(1, nrows // n):
        out = out + xm[s * n:(s + 1) * n]
    return out


def _mlstm(pb, gates, gate_bias, norm_w, batch, seq, chunk, group, init, hn_dtype):
    rows, width = pb.shape
    L = chunk
    t = np.arange(L)
    tril = (t[None, :] <= t[:, None]).astype(np.float32)
    tril2 = jnp.asarray(np.concatenate([tril, tril], axis=1), dtype=BF16)
    if group:
        per = L // group
        nc = 1
        blk = lambda b, c: (b // per, 0)
        out_rows = group
        hn_blk = lambda b, c: (b, 0)
    else:
        nc = seq // L
        blk = lambda b, c: (b * nc + c, 0)
        out_rows = L
        hn_blk = blk
    const = lambda b, c: (0, 0)
    state4 = lambda b, c: (b, 0, 0, 0)
    state3 = lambda b, c: (b, 0, 0)
    in_specs = [pl.BlockSpec((L, width), blk),
                pl.BlockSpec((L, LANES), blk),
                pl.BlockSpec((1, LANES), const),
                pl.BlockSpec((1, ML_WIDTH), const),
                pl.BlockSpec((L, 2 * L), const)]
    args = [pb, gates, gate_bias, norm_w.reshape(1, ML_WIDTH), tril2]
    c_spec = pl.BlockSpec((None, ML_HEADS, ML_V, ML_QK), state4)
    n_spec = pl.BlockSpec((None, ML_HEADS, 1, ML_QK), state4)
    m_spec = pl.BlockSpec((None, ML_HEADS, LANES), state3)
    if init is not None:
        in_specs += [c_spec, n_spec, m_spec]
        args += list(init)
    return pl.pallas_call(
        functools.partial(_mlstm_kernel, chunk=L, group=group, has_init=init is not None),
        grid=(batch, nc),
        in_specs=in_specs,
        out_specs=[pl.BlockSpec((out_rows, ML_WIDTH), hn_blk), c_spec, n_spec, m_spec],
        out_shape=[jax.ShapeDtypeStruct((rows, ML_WIDTH), hn_dtype),
                   jax.ShapeDtypeStruct((batch, ML_HEADS, ML_V, ML_QK), F32),
                   jax.ShapeDtypeStruct((batch, ML_HEADS, 1, ML_QK), F32),
                   jax.ShapeDtypeStruct((batch, ML_HEADS, LANES), F32)],
        scratch_shapes=[pltpu.VMEM((ML_HEADS, ML_V, ML_QK), F32),
                        pltpu.VMEM((ML_HEADS, 1, ML_QK), F32),
                        pltpu.VMEM((ML_HEADS, LANES), F32)],
        compiler_params=_params(("arbitrary", "arbitrary")),
        name="mlstm",
    )(*args)


def _layer(x, is_prompt, batch, seq, wts, extra):
    (norm_mix_w, w_in, w_gates, b_sb, gate_bias, ml_norm_w, w_out, norm_ffn_w, w_gate, w_up, w_down) = wts
    rows = x.shape[0]
    tm = 1024 if is_prompt else rows

    xn = _rmsnorm(x, norm_mix_w, BF16, min(tm, 256))
    (pb,) = _matmul(xn, w_in, 3 * SB_WIDTH, 3 * ML_WIDTH, tm, 512, [F32], "inproj_ml")
    (gates,) = _matmul(xn, w_gates, 0, LANES, tm, LANES, [F32], "inproj_gates")

    if is_prompt:
        (qt4,) = _matmul_t(xn, w_in, 0, SB_WIDTH, tm, 512, [], HEAD_DIM ** -0.5 * LOG2E, KV_BLK, "inproj_qt")
        k_a, k_b = _matmul(xn, w_in, SB_WIDTH, SB_WIDTH, tm, 512, [F32, BF16], "inproj_k")
        v_a, vt4 = _matmul_t(xn, w_in, 2 * SB_WIDTH, SB_WIDTH, tm, 512, [F32], 1.0, KV_BLK, "inproj_vt")
        o_a = _sb_prompt(qt4, k_b, vt4, b_sb, batch, seq, 512, 4, BF16)
        hn, c, n, m = _mlstm(pb, gates, gate_bias, ml_norm_w, batch, seq, 128, 0, None, BF16)
    else:
        (q_a,) = _matmul(xn, w_in, 0, SB_WIDTH, tm, 512, [F32], "inproj_q")
        (k_a,) = _matmul(xn, w_in, SB_WIDTH, SB_WIDTH, tm, 512, [F32], "inproj_k")
        (v_a,) = _matmul(xn, w_in, 2 * SB_WIDTH, SB_WIDTH, tm, 512, [F32], "inproj_v")
        cache_k, cache_v, page_table, init = extra
        o_a = _sb_sample(q_a, k_a, v_a, b_sb, cache_k, cache_v, page_table, seq, 4)
        hn, c, n, m = _mlstm(pb, gates, gate_bias, ml_norm_w, batch, seq, 128, seq, init, F32)

    x1 = _outproj(o_a, hn, w_out, x, tm, 512)
    xn2 = _rmsnorm(x1, norm_ffn_w, BF16, min(tm, 256))
    hid = _gateup(xn2, w_gate, w_up, tm, 256)
    x2 = _down(hid, w_down, x1, min(tm, 512), 256)
    return x2, k_a, v_a, c, n, m


def kernel(x_prompt, x_sample, cache_k, cache_v, page_table, state_c, state_n, state_m, norm_mix_w, w_in, b_sb,
           b_igate, b_fgate, ml_norm_w, w_out, norm_ffn_w, w_gate, w_up, w_down, final_norm_w):
    depth = w_in.shape[0]
    assert depth == 1
    bp, sp, d = x_prompt.shape
    bs, ss, _ = x_sample.shape
    n_main = 3 * SB_WIDTH + 3 * ML_WIDTH
    l = 0
    w_in_b = w_in[l].astype(BF16)
    w_gates = jnp.pad(w_in_b[:, n_main:], ((0, 0), (0, LANES - 2 * ML_HEADS)))
    gate_bias = jnp.concatenate([b_igate[l].astype(F32), b_fgate[l].astype(F32),
                                 jnp.zeros((LANES - 2 * ML_HEADS,), F32)])[None, :]
    wts = (norm_mix_w[l], w_in_b, w_gates, b_sb[l], gate_bias, ml_norm_w[l], w_out[l].astype(BF16),
           norm_ffn_w[l], w_gate[l].astype(BF16), w_up[l].astype(BF16), w_down[l].astype(BF16))

    xp, kp, vp, cp, np_, mp = _layer(x_prompt.reshape(bp * sp, d), True, bp, sp, wts, None)

    init = (state_c[l], state_n[l][:, :, None, :],
            jnp.broadcast_to(state_m[l][:, :, None], (bs, ML_HEADS, LANES)))
    extra = (cache_k[l], cache_v[l],
             page_table, init)
    xs, ks, vs, cs, ns, ms = _layer(x_sample.reshape(bs * ss, d), False, bs, ss, wts, extra)

    y_prompt = _rmsnorm(xp, final_norm_w, F32, 256).reshape(bp, sp, d)
    y_sample = _rmsnorm(xs, final_norm_w, F32, 256).reshape(bs, ss, d)
    return (y_prompt, y_sample,
            kp.reshape(1, bp, sp, SB_HEADS, HEAD_DIM), vp.reshape(1, bp, sp, SB_HEADS, HEAD_DIM),
            cp[None], np_[:, :, 0, :][None], mp[:, :, 0][None],
            ks.reshape(1, bs, ss, SB_HEADS, HEAD_DIM), vs.reshape(1, bs, ss, SB_HEADS, HEAD_DIM),
            cs[None], ns[:, :, 0, :][None], ms[:, :, 0][None])
```

```python
import functools
import math

import numpy as np
import jax
import jax.numpy as jnp
from jax import lax
from jax.experimental import pallas as pl
from jax.experimental.pallas import tpu as pltpu

F32 = jnp.float32
BF16 = jnp.bfloat16
EPS = 1e-6

HEAD_DIM = 128
SB_HEADS = 16
SB_WIDTH = SB_HEADS * HEAD_DIM
ML_HEADS = 4
ML_QK = 256
ML_V = 512
ML_WIDTH = ML_HEADS * ML_V
PAGE = 128
PITCH = 24
PAGE_SETS = 3
LANES = 128
SUBLANES = 8

VMEM_LIMIT = 56 * 1024 * 1024


def _params(sem):
    return pltpu.CompilerParams(dimension_semantics=sem, vmem_limit_bytes=VMEM_LIMIT)


def _log_sigmoid(x):
    return jnp.minimum(x, 0.0) - jnp.log(1.0 + jnp.exp(-jnp.abs(x)))


def _div_pow2(x, n):
    assert n & (n - 1) == 0
    return lax.shift_right_logical(x, int(math.log2(n)))


def _split_bf16(x):
    hi = x.astype(BF16)
    lo = (x - hi.astype(F32)).astype(BF16)
    return hi, lo


def _rmsnorm_kernel(x_ref, w_ref, o_ref):
    x = x_ref[...]
    ms = jnp.mean(x * x, axis=-1, keepdims=True)
    o_ref[...] = (x * lax.rsqrt(ms + EPS) * w_ref[...]).astype(o_ref.dtype)


def _rmsnorm(x, w, out_dtype, tm):
    m, d = x.shape
    return pl.pallas_call(
        _rmsnorm_kernel,
        grid=(m // tm,),
        in_specs=[pl.BlockSpec((tm, d), lambda i: (i, 0)),
                  pl.BlockSpec((1, d), lambda i: (0, 0))],
        out_specs=pl.BlockSpec((tm, d), lambda i: (i, 0)),
        out_shape=jax.ShapeDtypeStruct((m, d), out_dtype),
        compiler_params=_params(("arbitrary",)),
        name="rmsnorm",
    )(x, w.reshape(1, d))


def _mm_kernel(a_ref, w_ref, *o_refs):
    acc = jnp.dot(a_ref[...].astype(BF16), w_ref[...], preferred_element_type=F32)
    for o_ref in o_refs:
        o_ref[...] = acc.astype(o_ref.dtype)


def _matmul(a, w, col_off, n_cols, tm, tn, out_dtypes, name):
    m, k = a.shape
    assert col_off % tn == 0 and n_cols % tn == 0 and m % tm == 0
    off = col_off // tn
    outs = pl.pallas_call(
        _mm_kernel,
        grid=(m // tm, n_cols // tn),
        in_specs=[pl.BlockSpec((tm, k), lambda i, j: (i, 0)),
                  pl.BlockSpec((k, tn), lambda i, j: (0, j + off))],
        out_specs=[pl.BlockSpec((tm, tn), lambda i, j: (i, j)) for _ in out_dtypes],
        out_shape=[jax.ShapeDtypeStruct((m, n_cols), dt) for dt in out_dtypes],
        compiler_params=_params(("arbitrary", "arbitrary")),
        name=name,
    )(a, w)
    return outs


def _mm_t_kernel(a_ref, w_ref, *o_refs, scale, tb):
    acc = jnp.dot(a_ref[...], w_ref[...], preferred_element_type=F32)
    for o_ref in o_refs[:-1]:
        o_ref[...] = acc.astype(o_ref.dtype)
    t_ref = o_refs[-1]
    tm, tn = acc.shape
    for blk in range(tm // tb):
        for hh in range(tn // HEAD_DIM):
            piece = acc[blk * tb:(blk + 1) * tb, hh * HEAD_DIM:(hh + 1) * HEAD_DIM] * scale
            t_ref[blk, hh] = piece.T.astype(t_ref.dtype)


def _matmul_t(a, w, col_off, n_cols, tm, tn, out_dtypes, scale, tb, name):
    m, k = a.shape
    assert col_off % tn == 0 and n_cols % tn == 0 and m % tm == 0 and tm % tb == 0 and tn % HEAD_DIM == 0
    off = col_off // tn
    heads = n_cols // HEAD_DIM
    return pl.pallas_call(
        functools.partial(_mm_t_kernel, scale=scale, tb=tb),
        grid=(m // tm, n_cols // tn),
        in_specs=[pl.BlockSpec((tm, k), lambda i, j: (i, 0)),
                  pl.BlockSpec((k, tn), lambda i, j: (0, j + off))],
        out_specs=[pl.BlockSpec((tm, tn), lambda i, j: (i, j)) for _ in out_dtypes]
                  + [pl.BlockSpec((tm // tb, tn // HEAD_DIM, HEAD_DIM, tb), lambda i, j: (i, j, 0, 0))],
        out_shape=[jax.ShapeDtypeStruct((m, n_cols), dt) for dt in out_dtypes]
                  + [jax.ShapeDtypeStruct((m // tb, heads, HEAD_DIM, tb), BF16)],
        compiler_params=_params(("arbitrary", "arbitrary")),
        name=name,
    )(a, w)


def _outproj_kernel(a1_ref, a2_ref, w1_ref, w2_ref, r_ref, o_ref):
    acc = jnp.dot(a1_ref[...].astype(BF16), w1_ref[...], preferred_element_type=F32)
    acc += jnp.dot(a2_ref[...].astype(BF16), w2_ref[...], preferred_element_type=F32)
    o_ref[...] = r_ref[...] + acc


def _outproj(a1, a2, w, res, tm, tn):
    m, k1 = a1.shape
    k2 = a2.shape[1]
    n = w.shape[1]
    assert k1 == k2
    return pl.pallas_call(
        _outproj_kernel,
        grid=(m // tm, n // tn),
        in_specs=[pl.BlockSpec((tm, k1), lambda i, j: (i, 0)),
                  pl.BlockSpec((tm, k2), lambda i, j: (i, 0)),
                  pl.BlockSpec((k1, tn), lambda i, j: (0, j)),
                  pl.BlockSpec((k2, tn), lambda i, j: (1, j)),
                  pl.BlockSpec((tm, tn), lambda i, j: (i, j))],
        out_specs=pl.BlockSpec((tm, tn), lambda i, j: (i, j)),
        out_shape=jax.ShapeDtypeStruct((m, n), F32),
        compiler_params=_params(("arbitrary", "arbitrary")),
        name="outproj",
    )(a1, a2, w, w, res)


def _gateup_kernel(a_ref, wg_ref, wu_ref, o_ref):
    a = a_ref[...]
    g = jnp.dot(a, wg_ref[...], preferred_element_type=F32)
    u = jnp.dot(a, wu_ref[...], preferred_element_type=F32)
    o_ref[...] = (g / (1.0 + jnp.exp(-g)) * u).astype(o_ref.dtype)


def _gateup(a, wg, wu, tm, tn):
    m, k = a.shape
    f = wg.shape[1]
    assert f % tn == 0
    return pl.pallas_call(
        _gateup_kernel,
        grid=(m // tm, f // tn),
        in_specs=[pl.BlockSpec((tm, k), lambda i, j: (i, 0)),
                  pl.BlockSpec((k, tn), lambda i, j: (0, j)),
                  pl.BlockSpec((k, tn), lambda i, j: (0, j))],
        out_specs=pl.BlockSpec((tm, tn), lambda i, j: (i, j)),
        out_shape=jax.ShapeDtypeStruct((m, f), BF16),
        compiler_params=_params(("arbitrary", "arbitrary")),
        name="ffn_gateup",
    )(a, wg, wu)


def _down_kernel(a_ref, w_ref, r_ref, o_ref):
    o_ref[...] = r_ref[...] + jnp.dot(a_ref[...], w_ref[...], preferred_element_type=F32)


def _down(a, w, res, tm, tn):
    m, k = a.shape
    n = w.shape[1]
    return pl.pallas_call(
        _down_kernel,
        grid=(m // tm, n // tn),
        in_specs=[pl.BlockSpec((tm, k), lambda i, j: (i, 0)),
                  pl.BlockSpec((k, tn), lambda i, j: (0, j)),
                  pl.BlockSpec((tm, tn), lambda i, j: (i, j))],
        out_specs=pl.BlockSpec((tm, tn), lambda i, j: (i, j)),
        out_shape=jax.ShapeDtypeStruct((m, n), F32),
        compiler_params=_params(("arbitrary", "arbitrary")),
        name="ffn_down",
    )(a, w, res)


KV_BLK = 256
TOT_ROWS = 16
LOG2E = 1.4426950408889634


def _neg_abs(x):
    bits = pltpu.bitcast(x, jnp.uint32) | jnp.uint32(0x80000000)
    return pltpu.bitcast(bits, F32)


def _sb_prompt_kernel(qt_ref, k_ref, vt_ref, b_ref, tri_ref, o_ref, ot_acc, *, tq, hpg):
    qi = pl.program_id(2)
    nqb = tq // KV_BLK
    qts = [jnp.concatenate([qt_ref[i, h] for i in range(nqb)], axis=1) for h in range(hpg)]
    tri = tri_ref[...]
    ot_acc[...] = jnp.zeros_like(ot_acc)

    def blocks(base, carry, masked):
        js = [base + u for u in reversed(range(nqb))]
        if masked:
            qpos = qi * tq + lax.broadcasted_iota(jnp.int32, (KV_BLK, tq), 1)
            valids = [(j * KV_BLK + lax.broadcasted_iota(jnp.int32, (KV_BLK, tq), 0)) < qpos for j in js]
        log_betas, log_keeps = [], []
        for h in range(hpg):
            for n, j in enumerate(js):
                kb = k_ref[pl.ds(pl.multiple_of(j * KV_BLK, KV_BLK), KV_BLK), h * HEAD_DIM:(h + 1) * HEAD_DIM]
                z = jnp.dot(kb, qts[h], preferred_element_type=F32) + b_ref[h]
                l = jnp.log(1.0 + jnp.exp2(_neg_abs(z))) * LOG2E
                log_beta = jnp.minimum(z, 0.0) - l
                log_keep = log_beta - z
                if masked:
                    log_keep = jnp.where(valids[n], log_keep, 0.0)
                log_betas.append(log_beta)
                log_keeps.append(log_keep.astype(BF16))
        cums = [jnp.dot(tri, lk, preferred_element_type=F32) for lk in log_keeps]
        new_carry = []
        for h in range(hpg):
            c = carry[h]
            ws = []
            for n in range(nqb):
                i = h * nqb + n
                w = jnp.exp2(log_betas[i] + cums[i][:KV_BLK] + c)
                if masked:
                    w = jnp.where(valids[n], w, 0.0)
                ws.append(w.astype(BF16))
                c = c + cums[i][KV_BLK:KV_BLK + 1]
            new_carry.append(c)
            vt = jnp.concatenate([vt_ref[j, h] for j in js], axis=1)
            ot_acc[h] += jnp.dot(vt, jnp.concatenate(ws, axis=0), preferred_element_type=F32)
        return tuple(new_carry)

    zero = tuple(jnp.zeros((1, tq), F32) for _ in range(hpg))
    carry = blocks(qi * nqb, zero, True)
    lax.fori_loop(0, qi, lambda i, c: blocks((qi - 1 - i) * nqb, c, False), carry)
    for h in range(hpg):
        o_ref[:, h * HEAD_DIM:(h + 1) * HEAD_DIM] = ot_acc[h].T.astype(o_ref.dtype)


def _later_key_matrix():
    s = np.arange(KV_BLK)[:, None]
    j = np.arange(KV_BLK)[None, :]
    tri = np.concatenate([(j > s).astype(np.float32), np.ones((TOT_ROWS, KV_BLK), np.float32)], axis=0)
    return jnp.asarray(tri, dtype=BF16)


def _sb_prompt(qt4, k, vt4, b_sb, batch, seq, tq, hpg, out_dtype):
    heads = k.shape[1] // HEAD_DIM
    assert seq % tq == 0 and tq % KV_BLK == 0 and heads % hpg == 0
    nq = seq // tq
    nqb = tq // KV_BLK
    nkb = seq // KV_BLK
    b_rep = jnp.broadcast_to((b_sb.astype(F32) * LOG2E)[:, None, None], (heads, 1, tq))
    return pl.pallas_call(
        functools.partial(_sb_prompt_kernel, tq=tq, hpg=hpg),
        grid=(batch, heads // hpg, nq),
        in_specs=[pl.BlockSpec((nqb, hpg, HEAD_DIM, KV_BLK), lambda b, g, qi: (b * nq + qi, g, 0, 0)),
                  pl.BlockSpec((seq, hpg * HEAD_DIM), lambda b, g, qi: (b, g)),
                  pl.BlockSpec((nkb, hpg, HEAD_DIM, KV_BLK), lambda b, g, qi: (b, g, 0, 0)),
                  pl.BlockSpec((hpg, 1, tq), lambda b, g, qi: (g, 0, 0)),
                  pl.BlockSpec((KV_BLK + TOT_ROWS, KV_BLK), lambda b, g, qi: (0, 0))],
        out_specs=pl.BlockSpec((tq, hpg * HEAD_DIM), lambda b, g, qi: (b * nq + qi, g)),
        out_shape=jax.ShapeDtypeStruct(k.shape, out_dtype),
        scratch_shapes=[pltpu.VMEM((hpg, HEAD_DIM, tq), F32)],
        compiler_params=_params(("arbitrary", "arbitrary", "arbitrary")),
        name="sb_prompt",
    )(qt4, k, vt4, b_rep, _later_key_matrix())


def _sb_sample_kernel(pt_ref, q_ref, kn_ref, vn_ref, b_ref, tri_ref, ck_hbm, cv_hbm, o_ref,
                      qbd, acc_o, acc_l, kbuf, vbuf, sems, *, pages_per_step, t_new, n_pages):
    npg = pages_per_step
    b = pl.program_id(0)
    g = pl.program_id(1)
    steps = n_pages // npg
    n = b * steps + g
    total = pl.num_programs(0) * steps
    heads = SB_WIDTH // HEAD_DIM
    bias = b_ref[...]
    tri = tri_ref[...]

    def page_copies(m):
        bm = m // steps
        gm = m - bm * steps
        first = (m % PAGE_SETS) * npg
        copies = []
        for i in range(npg):
            page = pt_ref[bm, n_pages - 1 - (gm * npg + i)]
            for src, buf, kv in ((ck_hbm, kbuf, 0), (cv_hbm, vbuf, 1)):
                copies.append(pltpu.make_async_copy(src.at[page], buf.at[first + i, :, pl.ds(0, heads), :],
                                                    sems.at[kv, first + i]))
        return copies

    ahead = PAGE_SETS - 1

    @pl.when(n == 0)
    def _():
        for m in range(ahead):
            for c in page_copies(m):
                c.start()

    @pl.when(n + ahead < total)
    def _():
        for c in page_copies(n + ahead):
            c.start()

    def process(kps, vps, valid):
        kall = jnp.concatenate(kps, axis=0) if len(kps) > 1 else kps[0]
        vall = jnp.concatenate(vps, axis=0) if len(vps) > 1 else vps[0]
        z = lax.dot_general(kall, qbd[...], (((1,), (1,)), ((), ())), preferred_element_type=F32) + bias
        log_beta = _log_sigmoid(z)
        log_keep = log_beta - z
        if valid is not None:
            log_keep = jnp.where(valid, log_keep, 0.0)
        carry = acc_l[...]
        afters = []
        for i in range(len(kps)):
            lk = log_keep[i * PAGE:(i + 1) * PAGE]
            hi, lo = _split_bf16(lk)
            afters.append(jnp.dot(tri, jnp.concatenate([hi, lo], axis=0), preferred_element_type=F32) + carry)
            carry = carry + jnp.sum(lk, axis=0, keepdims=True)
        acc_l[...] = carry
        w = jnp.exp(log_beta + (jnp.concatenate(afters, axis=0) if len(afters) > 1 else afters[0]))
        if valid is not None:
            w = jnp.where(valid, w, 0.0)
        acc_o[...] += jnp.dot(w.T.astype(BF16), vall, preferred_element_type=F32)

    @pl.when(g == 0)
    def _():
        rows = heads * t_new
        qt = jnp.concatenate([q_ref[...]] * heads, axis=0)
        rh = _div_pow2(lax.broadcasted_iota(jnp.int32, (rows, SB_WIDTH), 0), t_new)
        ch = _div_pow2(lax.broadcasted_iota(jnp.int32, (rows, SB_WIDTH), 1), HEAD_DIM)
        qbd[...] = jnp.where(rh == ch, qt * (HEAD_DIM ** -0.5), 0.0).astype(BF16)
        acc_o[...] = jnp.zeros_like(acc_o)
        acc_l[...] = jnp.zeros_like(acc_l)
        pad = jnp.zeros((PAGE - t_new, SB_WIDTH), F32)
        kn = jnp.concatenate([kn_ref[...], pad], axis=0).astype(BF16)
        vn = jnp.concatenate([vn_ref[...], pad], axis=0).astype(BF16)
        s_idx = lax.broadcasted_iota(jnp.int32, (PAGE, rows), 0)
        t_idx = lax.broadcasted_iota(jnp.int32, (PAGE, rows), 1) & (t_new - 1)
        process([kn], [vn], s_idx < t_idx)

    def page(buf, idx):
        flat = buf.at[idx].reshape(PAGE * PITCH, HEAD_DIM)
        return jnp.concatenate([flat[pl.ds(h, PAGE, stride=PITCH), :] for h in range(heads)], axis=1).astype(BF16)

    for c in page_copies(n):
        c.wait()
    first = (n % PAGE_SETS) * npg
    process([page(kbuf, first + i) for i in range(npg)], [page(vbuf, first + i) for i in range(npg)], None)

    @pl.when(g == pl.num_programs(1) - 1)
    def _():
        for h in range(heads):
            o_ref[:, h * HEAD_DIM:(h + 1) * HEAD_DIM] = acc_o[h * t_new:(h + 1) * t_new,
                                                              h * HEAD_DIM:(h + 1) * HEAD_DIM]


def _sb_sample(q, k_new, v_new, b_sb, cache_k, cache_v, page_table, t_new, pages_per_step):
    batch, n_pages = page_table.shape
    assert n_pages % pages_per_step == 0 and t_new == SUBLANES and SB_HEADS * t_new == LANES
    steps = n_pages // pages_per_step
    j = np.arange(PAGE)
    tm = (j[None, :] > j[:, None]).astype(np.float32)
    tri = jnp.asarray(np.concatenate([tm, tm], axis=1), dtype=BF16)
    bias = jnp.repeat(b_sb.astype(F32), t_new)[None, :]
    row_spec = pl.BlockSpec((t_new, SB_WIDTH), lambda b, g, pt: (b, 0))
    assert batch * steps >= PAGE_SETS - 1
    n_bufs = PAGE_SETS * pages_per_step
    grid_spec = pltpu.PrefetchScalarGridSpec(
        num_scalar_prefetch=1,
        grid=(batch, steps),
        in_specs=[row_spec, row_spec, row_spec,
                  pl.BlockSpec((1, LANES), lambda b, g, pt: (0, 0)),
                  pl.BlockSpec((PAGE, 2 * PAGE), lambda b, g, pt: (0, 0)),
                  pl.BlockSpec(memory_space=pl.ANY),
                  pl.BlockSpec(memory_space=pl.ANY)],
        out_specs=row_spec,
        scratch_shapes=[pltpu.VMEM((LANES, SB_WIDTH), BF16),
                        pltpu.VMEM((LANES, SB_WIDTH), F32),
                        pltpu.VMEM((1, LANES), F32),
                        pltpu.VMEM((n_bufs, PAGE, PITCH, HEAD_DIM), F32),
                        pltpu.VMEM((n_bufs, PAGE, PITCH, HEAD_DIM), F32),
                        pltpu.SemaphoreType.DMA((2, n_bufs))],
    )
    return pl.pallas_call(
        functools.partial(_sb_sample_kernel, pages_per_step=pages_per_step, t_new=t_new, n_pages=n_pages),
        grid_spec=grid_spec,
        out_shape=jax.ShapeDtypeStruct(q.shape, F32),
        compiler_params=_params(("arbitrary", "arbitrary")),
        name="sb_sample",
    )(page_table, q, k_new, v_new, bias, tri, cache_k, cache_v)


def _mlstm_kernel(*refs, chunk, group, has_init):
    if has_init:
        pb_ref, g_ref, gb_ref, nw_ref, tril_ref, c0_ref, n0_ref, m0_ref = refs[:8]
        refs = refs[8:]
    else:
        pb_ref, g_ref, gb_ref, nw_ref, tril_ref = refs[:5]
        refs = refs[5:]
    hn_ref, c_out, n_out, m_out, c_s, n_s, m_s = refs
    L = chunk
    ci = pl.program_id(1)

    @pl.when(ci == 0)
    def _():
        if has_init:
            c_s[...] = c0_ref[...]
            n_s[...] = n0_ref[...]
            m_s[...] = m0_ref[...]
        else:
            c_s[...] = jnp.zeros_like(c_s)
            n_s[...] = jnp.zeros_like(n_s)
            m_s[...] = jnp.zeros_like(m_s)

    row = lax.broadcasted_iota(jnp.int32, (L, L), 0)
    col = lax.broadcasted_iota(jnp.int32, (L, L), 1)
    mask = col <= row
    g = g_ref[...] + gb_ref[...]
    lf = _log_sigmoid(g)
    if group:
        live = pl.program_id(0) & (L // group - 1)
        rlive = _div_pow2(lax.broadcasted_iota(jnp.int32, (L, 1), 0), group) == live
        mask = jnp.logical_and(mask, _div_pow2(col, group) == live)
        lf = jnp.where(rlive, lf, 0.0)
    hi, lo = _split_bf16(lf)
    bsum = jnp.dot(tril_ref[...], jnp.concatenate([hi, lo], axis=0), preferred_element_type=F32)
    g_t = g.T
    bsum_t = bsum.T

    for h in range(ML_HEADS):
        qf = pb_ref[:, h * ML_QK:(h + 1) * ML_QK]
        kf = pb_ref[:, ML_HEADS * ML_QK + h * ML_QK:ML_HEADS * ML_QK + (h + 1) * ML_QK] * (ML_QK ** -0.5)
        v0 = 2 * ML_HEADS * ML_QK
        vf = pb_ref[:, v0 + h * ML_V:v0 + (h + 1) * ML_V]
        ob = pb_ref[:, v0 + ML_WIDTH + h * ML_V:v0 + ML_WIDTH + (h + 1) * ML_V]
        q = qf.astype(BF16)
        k = kf.astype(BF16)
        b_col = bsum[:, ML_HEADS + h:ML_HEADS + h + 1]
        b_row = bsum_t[ML_HEADS + h:ML_HEADS + h + 1, :]
        i_col = g[:, h:h + 1]
        i_row = g_t[h:h + 1, :]
        m_prev = m_s[h:h + 1, 0:1]
        c_st = c_s[h]
        n_st = n_s[h]

        d = jnp.where(mask, b_col - b_row + i_row, -jnp.inf)
        inter = b_col + m_prev
        m_t = jnp.maximum(inter, jnp.max(d, axis=1, keepdims=True))
        w_inter = jnp.exp(inter - m_t)
        w_intra = jnp.exp(d - m_t) * lax.dot_general(q, k, (((1,), (1,)), ((), ())), preferred_element_type=F32)
        qc = lax.dot_general(q, c_st.astype(BF16), (((1,), (1,)), ((), ())), preferred_element_type=F32)
        num = w_inter * qc + jnp.dot(w_intra.astype(BF16), vf.astype(BF16), preferred_element_type=F32)
        den = w_inter * jnp.sum(qf * n_st, axis=1, keepdims=True) + jnp.sum(w_intra, axis=1, keepdims=True)
        hh = num / jnp.maximum(jnp.abs(den), jnp.exp(-m_t))

        b_last = b_col[L - 1:L, :]
        m_new = m_t[L - 1:L, :]
        g_state = jnp.exp(b_last + m_prev - m_new)
        g_rows = jnp.exp(b_last - b_col + i_col - m_new)
        if group:
            g_rows = jnp.where(rlive, g_rows, 0.0)
        gv_t = (g_rows * vf).T.astype(BF16)
        c_s[h] = g_state * c_st + jnp.dot(gv_t, k, preferred_element_type=F32)
        n_s[h] = g_state * n_st + jnp.sum(g_rows * kf, axis=0, keepdims=True)
        m_s[h:h + 1, :] = jnp.broadcast_to(m_new, (1, LANES))

        hn = hh * lax.rsqrt(jnp.mean(hh * hh, axis=1, keepdims=True) + EPS)
        hn = hn * nw_ref[:, h * ML_V:(h + 1) * ML_V] / (1.0 + jnp.exp(-ob))
        if group:
            hn_ref[:, h * ML_V:(h + 1) * ML_V] = _rows(hn, live * group, group).astype(hn_ref.dtype)
        else:
            hn_ref[:, h * ML_V:(h + 1) * ML_V] = hn.astype(hn_ref.dtype)

    @pl.when(ci == pl.num_programs(1) - 1)
    def _():
        c_out[...] = c_s[...]
        n_out[...] = n_s[...]
        m_out[...] = m_s[...]


def _rows(x, r0, n):
    nrows = x.shape[0]
    ridx = lax.broadcasted_iota(jnp.int32, (nrows, 1), 0)
    xm = jnp.where(jnp.logical_and(ridx >= r0, ridx < r0 + n), x, 0.0)
    out = xm[0:n]
    for s in range(1, nrows // n):
        out = out + xm[s * n:(s + 1) * n]
    return out


def _mlstm(pb, gates, gate_bias, norm_w, batch, seq, chunk, group, init, hn_dtype):
    rows, width = pb.shape
    L = chunk
    t = np.arange(L)
    tril = (t[None, :] <= t[:, None]).astype(np.float32)
    tril2 = jnp.asarray(np.concatenate([tril, tril], axis=1), dtype=BF16)
    if group:
        per = L // group
        nc = 1
        blk = lambda b, c: (b // per, 0)
        out_rows = group
        hn_blk = lambda b, c: (b, 0)
    else:
        nc = seq // L
        blk = lambda b, c: (b * nc + c, 0)
        out_rows = L
        hn_blk = blk
    const = lambda b, c: (0, 0)
    state4 = lambda b, c: (b, 0, 0, 0)
    state3 = lambda b, c: (b, 0, 0)
    in_specs = [pl.BlockSpec((L, width), blk),
                pl.BlockSpec((L, LANES), blk),
                pl.BlockSpec((1, LANES), const),
                pl.BlockSpec((1, ML_WIDTH), const),
                pl.BlockSpec((L, 2 * L), const)]
    args = [pb, gates, gate_bias, norm_w.reshape(1, ML_WIDTH), tril2]
    c_spec = pl.BlockSpec((None, ML_HEADS, ML_V, ML_QK), state4)
    n_spec = pl.BlockSpec((None, ML_HEADS, 1, ML_QK), state4)
    m_spec = pl.BlockSpec((None, ML_HEADS, LANES), state3)
    if init is not None:
        in_specs += [c_spec, n_spec, m_spec]
        args += list(init)
    return pl.pallas_call(
        functools.partial(_mlstm_kernel, chunk=L, group=group, has_init=init is not None),
        grid=(batch, nc),
        in_specs=in_specs,
        out_specs=[pl.BlockSpec((out_rows, ML_WIDTH), hn_blk), c_spec, n_spec, m_spec],
        out_shape=[jax.ShapeDtypeStruct((rows, ML_WIDTH), hn_dtype),
                   jax.ShapeDtypeStruct((batch, ML_HEADS, ML_V, ML_QK), F32),
                   jax.ShapeDtypeStruct((batch, ML_HEADS, 1, ML_QK), F32),
                   jax.ShapeDtypeStruct((batch, ML_HEADS, LANES), F32)],
        scratch_shapes=[pltpu.VMEM((ML_HEADS, ML_V, ML_QK), F32),
                        pltpu.VMEM((ML_HEADS, 1, ML_QK), F32),
                        pltpu.VMEM((ML_HEADS, LANES), F32)],
        compiler_params=_params(("arbitrary", "arbitrary")),
        name="mlstm",
    )(*args)


def _layer(x, is_prompt, batch, seq, wts, extra):
    (norm_mix_w, w_in, w_gates, b_sb, gate_bias, ml_norm_w, w_out, norm_ffn_w, w_gate, w_up, w_down) = wts
    rows = x.shape[0]
    tm = 1024 if is_prompt else rows

    xn = _rmsnorm(x, norm_mix_w, BF16, min(tm, 256))
    (pb,) = _matmul(xn, w_in, 3 * SB_WIDTH, 3 * ML_WIDTH, tm, 512, [F32], "inproj_ml")
    (gates,) = _matmul(xn, w_gates, 0, LANES, tm, LANES, [F32], "inproj_gates")

    if is_prompt:
        (qt4,) = _matmul_t(xn, w_in, 0, SB_WIDTH, tm, 512, [], HEAD_DIM ** -0.5 * LOG2E, KV_BLK, "inproj_qt")
        k_a, k_b = _matmul(xn, w_in, SB_WIDTH, SB_WIDTH, tm, 512, [F32, BF16], "inproj_k")
        v_a, vt4 = _matmul_t(xn, w_in, 2 * SB_WIDTH, SB_WIDTH, tm, 512, [F32], 1.0, KV_BLK, "inproj_vt")
        o_a = _sb_prompt(qt4, k_b, vt4, b_sb, batch, seq, 512, 4, BF16)
        hn, c, n, m = _mlstm(pb, gates, gate_bias, ml_norm_w, batch, seq, 128, 0, None, BF16)
    else:
        (q_a,) = _matmul(xn, w_in, 0, SB_WIDTH, tm, 512, [F32], "inproj_q")
        (k_a,) = _matmul(xn, w_in, SB_WIDTH, SB_WIDTH, tm, 512, [F32], "inproj_k")
        (v_a,) = _matmul(xn, w_in, 2 * SB_WIDTH, SB_WIDTH, tm, 512, [F32], "inproj_v")
        cache_k, cache_v, page_table, init = extra
        o_a = _sb_sample(q_a, k_a, v_a, b_sb, cache_k, cache_v, page_table, seq, 4)
        hn, c, n, m = _mlstm(pb, gates, gate_bias, ml_norm_w, batch, seq, 128, seq, init, F32)

    x1 = _outproj(o_a, hn, w_out, x, tm, 512)
    xn2 = _rmsnorm(x1, norm_ffn_w, BF16, min(tm, 256))
    hid = _gateup(xn2, w_gate, w_up, tm, 256)
    x2 = _down(hid, w_down, x1, min(tm, 512), 256)
    return x2, k_a, v_a, c, n, m


def kernel(x_prompt, x_sample, cache_k, cache_v, page_table, state_c, state_n, state_m, norm_mix_w, w_in, b_sb,
           b_igate, b_fgate, ml_norm_w, w_out, norm_ffn_w, w_gate, w_up, w_down, final_norm_w):
    depth = w_in.shape[0]
    assert depth == 1
    bp, sp, d = x_prompt.shape
    bs, ss, _ = x_sample.shape
    n_main = 3 * SB_WIDTH + 3 * ML_WIDTH
    l = 0
    w_in_b = w_in[l].astype(BF16)
    w_gates = jnp.pad(w_in_b[:, n_main:], ((0, 0), (0, LANES - 2 * ML_HEADS)))
    gate_bias = jnp.concatenate([b_igate[l].astype(F32), b_fgate[l].astype(F32),
                                 jnp.zeros((LANES - 2 * ML_HEADS,), F32)])[None, :]
    wts = (norm_mix_w[l], w_in_b, w_gates, b_sb[l], gate_bias, ml_norm_w[l], w_out[l].astype(BF16),
           norm_ffn_w[l], w_gate[l].astype(BF16), w_up[l].astype(BF16), w_down[l].astype(BF16))

    xp, kp, vp, cp, np_, mp = _layer(x_prompt.reshape(bp * sp, d), True, bp, sp, wts, None)

    init = (state_c[l], state_n[l][:, :, None, :],
            jnp.broadcast_to(state_m[l][:, :, None], (bs, ML_HEADS, LANES)))
    extra = (cache_k[l], cache_v[l],
             page_table, init)
    xs, ks, vs, cs, ns, ms = _layer(x_sample.reshape(bs * ss, d), False, bs, ss, wts, extra)

    y_prompt = _rmsnorm(xp, final_norm_w, F32, 256).reshape(bp, sp, d)
    y_sample = _rmsnorm(xs, final_norm_w, F32, 256).reshape(bs, ss, d)
    return (y_prompt, y_sample,
            kp.reshape(1, bp, sp, SB_HEADS, HEAD_DIM), vp.reshape(1, bp, sp, SB_HEADS, HEAD_DIM),
            cp[None], np_[:, :, 0, :][None], mp[:, :, 0][None],
            ks.reshape(1, bs, ss, SB_HEADS, HEAD_DIM), vs.reshape(1, bs, ss, SB_HEADS, HEAD_DIM),
            cs[None], ns[:, :, 0, :][None], ms[:, :, 0][None])
```

```python
import functools
import math

import numpy as np
import jax
import jax.numpy as jnp
from jax import lax
from jax.experimental import pallas as pl
from jax.experimental.pallas import tpu as pltpu

F32 = jnp.float32
BF16 = jnp.bfloat16
EPS = 1e-6

HEAD_DIM = 128
SB_HEADS = 16
SB_WIDTH = SB_HEADS * HEAD_DIM
ML_HEADS = 4
ML_QK = 256
ML_V = 512
ML_WIDTH = ML_HEADS * ML_V
PAGE = 128
PITCH = 24
PAGE_SETS = 3
LANES = 128
SUBLANES = 8

VMEM_LIMIT = 56 * 1024 * 1024


def _params(sem):
    return pltpu.CompilerParams(dimension_semantics=sem, vmem_limit_bytes=VMEM_LIMIT)


def _log_sigmoid(x):
    return jnp.minimum(x, 0.0) - jnp.log(1.0 + jnp.exp(-jnp.abs(x)))


def _div_pow2(x, n):
    assert n & (n - 1) == 0
    return lax.shift_right_logical(x, int(math.log2(n)))


def _split_bf16(x):
    hi = x.astype(BF16)
    lo = (x - hi.astype(F32)).astype(BF16)
    return hi, lo


def _rmsnorm_kernel(xp_ref, xs_ref, w_ref, op_ref, os_ref):
    def norm(x):
        ms = jnp.mean(x * x, axis=-1, keepdims=True)
        return x * lax.rsqrt(ms + EPS) * w_ref[...]

    op_ref[...] = norm(xp_ref[...]).astype(op_ref.dtype)

    @pl.when(pl.program_id(0) == 0)
    def _():
        os_ref[...] = norm(xs_ref[...]).astype(os_ref.dtype)


def _rmsnorm(xp, xs, w, out_dtype, tm):
    mp, d = xp.shape
    ms = xs.shape[0]
    return pl.pallas_call(
        _rmsnorm_kernel,
        grid=(mp // tm,),
        in_specs=[pl.BlockSpec((tm, d), lambda i: (i, 0)),
                  pl.BlockSpec((ms, d), lambda i: (0, 0)),
                  pl.BlockSpec((1, d), lambda i: (0, 0))],
        out_specs=[pl.BlockSpec((tm, d), lambda i: (i, 0)),
                   pl.BlockSpec((ms, d), lambda i: (0, 0))],
        out_shape=[jax.ShapeDtypeStruct((mp, d), out_dtype), jax.ShapeDtypeStruct((ms, d), out_dtype)],
        compiler_params=_params(("arbitrary",)),
        name="rmsnorm",
    )(xp, xs, w.reshape(1, d))


def _sample_block(ms, tn, nj):
    return pl.BlockSpec((ms, tn), lambda i, j: (0, jnp.where(i == 0, j, nj - 1)))


def _mm_kernel(ap_ref, as_ref, w_ref, *o_refs, n_plain, tb, scale):
    w = w_ref[...].astype(BF16)
    acc = jnp.dot(ap_ref[...], w, preferred_element_type=F32)
    for o_ref in o_refs[:n_plain]:
        o_ref[...] = acc.astype(o_ref.dtype)
    n_prompt = n_plain
    if tb:
        t_ref = o_refs[n_plain]
        n_prompt += 1
        tm, tn = acc.shape
        for blk in range(tm // tb):
            for hh in range(tn // HEAD_DIM):
                piece = acc[blk * tb:(blk + 1) * tb, hh * HEAD_DIM:(hh + 1) * HEAD_DIM] * scale
                t_ref[blk, hh] = piece.T.astype(t_ref.dtype)

    @pl.when(pl.program_id(0) == 0)
    def _():
        acc_s = jnp.dot(as_ref[...], w, preferred_element_type=F32)
        for o_ref in o_refs[n_prompt:]:
            o_ref[...] = acc_s.astype(o_ref.dtype)


def _matmul(ap, as_, w, col_off, n_cols, tm, tn, p_dtypes, s_dtypes, name, tb=0, scale=1.0):
    mp, k = ap.shape
    ms = as_.shape[0]
    assert col_off % tn == 0 and n_cols % tn == 0 and mp % tm == 0
    off = col_off // tn
    nj = n_cols // tn
    out_specs = [pl.BlockSpec((tm, tn), lambda i, j: (i, j)) for _ in p_dtypes]
    out_shape = [jax.ShapeDtypeStruct((mp, n_cols), dt) for dt in p_dtypes]
    if tb:
        assert tm % tb == 0 and tn % HEAD_DIM == 0
        out_specs.append(pl.BlockSpec((tm // tb, tn // HEAD_DIM, HEAD_DIM, tb), lambda i, j: (i, j, 0, 0)))
        out_shape.append(jax.ShapeDtypeStruct((mp // tb, n_cols // HEAD_DIM, HEAD_DIM, tb), BF16))
    out_specs += [_sample_block(ms, tn, nj) for _ in s_dtypes]
    out_shape += [jax.ShapeDtypeStruct((ms, n_cols), dt) for dt in s_dtypes]
    return pl.pallas_call(
        functools.partial(_mm_kernel, n_plain=len(p_dtypes), tb=tb, scale=scale),
        grid=(mp // tm, nj),
        in_specs=[pl.BlockSpec((tm, k), lambda i, j: (i, 0)),
                  pl.BlockSpec((ms, k), lambda i, j: (0, 0)),
                  pl.BlockSpec((k, tn), lambda i, j: (0, j + off))],
        out_specs=out_specs,
        out_shape=out_shape,
        compiler_params=_params(("arbitrary", "arbitrary")),
        name=name,
    )(ap, as_, w)


def _outproj_kernel(a1p_ref, a2p_ref, rp_ref, a1s_ref, a2s_ref, rs_ref, w1_ref, w2_ref, op_ref, os_ref):
    w1 = w1_ref[...].astype(BF16)
    w2 = w2_ref[...].astype(BF16)

    def proj(a1_ref, a2_ref, r_ref):
        acc = jnp.dot(a1_ref[...].astype(BF16), w1, preferred_element_type=F32)
        acc += jnp.dot(a2_ref[...].astype(BF16), w2, preferred_element_type=F32)
        return r_ref[...] + acc

    op_ref[...] = proj(a1p_ref, a2p_ref, rp_ref)

    @pl.when(pl.program_id(0) == 0)
    def _():
        os_ref[...] = proj(a1s_ref, a2s_ref, rs_ref)


def _outproj(a1p, a2p, resp, a1s, a2s, ress, w, tm, tn):
    mp, k1 = a1p.shape
    ms = a1s.shape[0]
    n = w.shape[1]
    nj = n // tn
    assert a2p.shape[1] == k1
    row = lambda i, j: (i, 0)
    whole = lambda i, j: (0, 0)
    return pl.pallas_call(
        _outproj_kernel,
        grid=(mp // tm, nj),
        in_specs=[pl.BlockSpec((tm, k1), row), pl.BlockSpec((tm, k1), row),
                  pl.BlockSpec((tm, tn), lambda i, j: (i, j)),
                  pl.BlockSpec((ms, k1), whole), pl.BlockSpec((ms, k1), whole),
                  _sample_block(ms, tn, nj),
                  pl.BlockSpec((k1, tn), lambda i, j: (0, j)),
                  pl.BlockSpec((k1, tn), lambda i, j: (1, j))],
        out_specs=[pl.BlockSpec((tm, tn), lambda i, j: (i, j)), _sample_block(ms, tn, nj)],
        out_shape=[jax.ShapeDtypeStruct((mp, n), F32), jax.ShapeDtypeStruct((ms, n), F32)],
        compiler_params=_params(("arbitrary", "arbitrary")),
        name="outproj",
    )(a1p, a2p, resp, a1s, a2s, ress, w, w)


def _gateup_kernel(ap_ref, as_ref, wg_ref, wu_ref, op_ref, os_ref):
    wg = wg_ref[...].astype(BF16)
    wu = wu_ref[...].astype(BF16)

    def swiglu(a):
        g = jnp.dot(a, wg, preferred_element_type=F32)
        u = jnp.dot(a, wu, preferred_element_type=F32)
        return g / (1.0 + jnp.exp(-g)) * u

    op_ref[...] = swiglu(ap_ref[...]).astype(op_ref.dtype)

    @pl.when(pl.program_id(0) == 0)
    def _():
        os_ref[...] = swiglu(as_ref[...]).astype(os_ref.dtype)


def _gateup(ap, as_, wg, wu, tm, tn):
    mp, k = ap.shape
    ms = as_.shape[0]
    f = wg.shape[1]
    assert f % tn == 0
    nj = f // tn
    return pl.pallas_call(
        _gateup_kernel,
        grid=(mp // tm, nj),
        in_specs=[pl.BlockSpec((tm, k), lambda i, j: (i, 0)),
                  pl.BlockSpec((ms, k), lambda i, j: (0, 0)),
                  pl.BlockSpec((k, tn), lambda i, j: (0, j)),
                  pl.BlockSpec((k, tn), lambda i, j: (0, j))],
        out_specs=[pl.BlockSpec((tm, tn), lambda i, j: (i, j)), _sample_block(ms, tn, nj)],
        out_shape=[jax.ShapeDtypeStruct((mp, f), BF16), jax.ShapeDtypeStruct((ms, f), BF16)],
        compiler_params=_params(("arbitrary", "arbitrary")),
        name="ffn_gateup",
    )(ap, as_, wg, wu)


def _down_kernel(ap_ref, rp_ref, as_ref, rs_ref, w_ref, op_ref, os_ref):
    w = w_ref[...]
    op_ref[...] = rp_ref[...] + jnp.dot(ap_ref[...], w, preferred_element_type=F32)

    @pl.when(pl.program_id(0) == 0)
    def _():
        os_ref[...] = rs_ref[...] + jnp.dot(as_ref[...], w, preferred_element_type=F32)


def _down(ap, resp, as_, ress, w, tm, tn):
    mp, k = ap.shape
    ms = as_.shape[0]
    n = w.shape[1]
    nj = n // tn
    return pl.pallas_call(
        _down_kernel,
        grid=(mp // tm, nj),
        in_specs=[pl.BlockSpec((tm, k), lambda i, j: (i, 0)),
                  pl.BlockSpec((tm, tn), lambda i, j: (i, j)),
                  pl.BlockSpec((ms, k), lambda i, j: (0, 0)),
                  _sample_block(ms, tn, nj),
                  pl.BlockSpec((k, tn), lambda i, j: (0, j))],
        out_specs=[pl.BlockSpec((tm, tn), lambda i, j: (i, j)), _sample_block(ms, tn, nj)],
        out_shape=[jax.ShapeDtypeStruct((mp, n), F32), jax.ShapeDtypeStruct((ms, n), F32)],
        compiler_params=_params(("arbitrary", "arbitrary")),
        name="ffn_down",
    )(ap, resp, as_, ress, w)


KV_BLK = 256
TOT_ROWS = 16
LOG2E = 1.4426950408889634


def _neg_abs(x):
    bits = pltpu.bitcast(x, jnp.uint32) | jnp.uint32(0x80000000)
    return pltpu.bitcast(bits, F32)


def _sb_prompt_kernel(qt_ref, k_ref, vt_ref, b_ref, tri_ref, o_ref, ot_acc, *, tq, hpg):
    qi = pl.program_id(2)
    nqb = tq // KV_BLK
    qts = [jnp.concatenate([qt_ref[i, h] for i in range(nqb)], axis=1) for h in range(hpg)]
    tri = tri_ref[...]
    ot_acc[...] = jnp.zeros_like(ot_acc)

    def blocks(base, carry, masked):
        js = [base + u for u in reversed(range(nqb))]
        if masked:
            qpos = qi * tq + lax.broadcasted_iota(jnp.int32, (KV_BLK, tq), 1)
            valids = [(j * KV_BLK + lax.broadcasted_iota(jnp.int32, (KV_BLK, tq), 0)) < qpos for j in js]
        log_betas, log_keeps = [], []
        for h in range(hpg):
            for n, j in enumerate(js):
                kb = k_ref[pl.ds(pl.multiple_of(j * KV_BLK, KV_BLK), KV_BLK), h * HEAD_DIM:(h + 1) * HEAD_DIM]
                z = jnp.dot(kb, qts[h], preferred_element_type=F32) + b_ref[h]
                l = jnp.log(1.0 + jnp.exp2(_neg_abs(z))) * LOG2E
                log_beta = jnp.minimum(z, 0.0) - l
                log_keep = log_beta - z
                if masked:
                    log_keep = jnp.where(valids[n], log_keep, 0.0)
                log_betas.append(log_beta)
                log_keeps.append(log_keep.astype(BF16))
        cums = [jnp.dot(tri, lk, preferred_element_type=F32) for lk in log_keeps]
        new_carry = []
        for h in range(hpg):
            c = carry[h]
            ws = []
            for n in range(nqb):
                i = h * nqb + n
                w = jnp.exp2(log_betas[i] + cums[i][:KV_BLK] + c)
                if masked:
                    w = jnp.where(valids[n], w, 0.0)
                ws.append(w.astype(BF16))
                c = c + cums[i][KV_BLK:KV_BLK + 1]
            new_carry.append(c)
            vt = jnp.concatenate([vt_ref[j, h] for j in js], axis=1)
            ot_acc[h] += jnp.dot(vt, jnp.concatenate(ws, axis=0), preferred_element_type=F32)
        return tuple(new_carry)

    zero = tuple(jnp.zeros((1, tq), F32) for _ in range(hpg))
    carry = blocks(qi * nqb, zero, True)
    lax.fori_loop(0, qi, lambda i, c: blocks((qi - 1 - i) * nqb, c, False), carry)
    for h in range(hpg):
        o_ref[:, h * HEAD_DIM:(h + 1) * HEAD_DIM] = ot_acc[h].T.astype(o_ref.dtype)


def _later_key_matrix():
    s = np.arange(KV_BLK)[:, None]
    j = np.arange(KV_BLK)[None, :]
    tri = np.concatenate([(j > s).astype(np.float32), np.ones((TOT_ROWS, KV_BLK), np.float32)], axis=0)
    return jnp.asarray(tri, dtype=BF16)


def _sb_prompt(qt4, k, vt4, b_sb, batch, seq, tq, hpg, out_dtype):
    heads = k.shape[1] // HEAD_DIM
    assert seq % tq == 0 and tq % KV_BLK == 0 and heads % hpg == 0
    nq = seq // tq
    nqb = tq // KV_BLK
    nkb = seq // KV_BLK
    b_rep = jnp.broadcast_to((b_sb.astype(F32) * LOG2E)[:, None, None], (heads, 1, tq))
    return pl.pallas_call(
        functools.partial(_sb_prompt_kernel, tq=tq, hpg=hpg),
        grid=(batch, heads // hpg, nq),
        in_specs=[pl.BlockSpec((nqb, hpg, HEAD_DIM, KV_BLK), lambda b, g, qi: (b * nq + qi, g, 0, 0)),
                  pl.BlockSpec((seq, hpg * HEAD_DIM), lambda b, g, qi: (b, g)),
                  pl.BlockSpec((nkb, hpg, HEAD_DIM, KV_BLK), lambda b, g, qi: (b, g, 0, 0)),
                  pl.BlockSpec((hpg, 1, tq), lambda b, g, qi: (g, 0, 0)),
                  pl.BlockSpec((KV_BLK + TOT_ROWS, KV_BLK), lambda b, g, qi: (0, 0))],
        out_specs=pl.BlockSpec((tq, hpg * HEAD_DIM), lambda b, g, qi: (b * nq + qi, g)),
        out_shape=jax.ShapeDtypeStruct(k.shape, out_dtype),
        scratch_shapes=[pltpu.VMEM((hpg, HEAD_DIM, tq), F32)],
        compiler_params=_params(("arbitrary", "arbitrary", "arbitrary")),
        name="sb_prompt",
    )(qt4, k, vt4, b_rep, _later_key_matrix())


def _sb_sample_kernel(pt_ref, q_ref, kn_ref, vn_ref, b_ref, tri_ref, ck_hbm, cv_hbm, o_ref,
                      qbd, acc_o, acc_l, kbuf, vbuf, sems, *, pages_per_step, t_new, n_pages):
    npg = pages_per_step
    b = pl.program_id(0)
    g = pl.program_id(1)
    steps = n_pages // npg
    n = b * steps + g
    total = pl.num_programs(0) * steps
    heads = SB_WIDTH // HEAD_DIM
    bias = b_ref[...]
    tri = tri_ref[...]

    def page_copies(m):
        bm = m // steps
        gm = m - bm * steps
        first = (m % PAGE_SETS) * npg
        copies = []
        for i in range(npg):
            page = pt_ref[bm, n_pages - 1 - (gm * npg + i)]
            for src, buf, kv in ((ck_hbm, kbuf, 0), (cv_hbm, vbuf, 1)):
                copies.append(pltpu.make_async_copy(src.at[page], buf.at[first + i, :, pl.ds(0, heads), :],
                                                    sems.at[kv, first + i]))
        return copies

    ahead = PAGE_SETS - 1

    @pl.when(n == 0)
    def _():
        for m in range(ahead):
            for c in page_copies(m):
                c.start()

    @pl.when(n + ahead < total)
    def _():
        for c in page_copies(n + ahead):
            c.start()

    def process(kps, vps, valid):
        kall = jnp.concatenate(kps, axis=0) if len(kps) > 1 else kps[0]
        vall = jnp.concatenate(vps, axis=0) if len(vps) > 1 else vps[0]
        z = lax.dot_general(kall, qbd[...], (((1,), (1,)), ((), ())), preferred_element_type=F32) + bias
        log_beta = _log_sigmoid(z)
        log_keep = log_beta - z
        if valid is not None:
            log_keep = jnp.where(valid, log_keep, 0.0)
        carry = acc_l[...]
        afters = []
        for i in range(len(kps)):
            lk = log_keep[i * PAGE:(i + 1) * PAGE]
            hi, lo = _split_bf16(lk)
            afters.append(jnp.dot(tri, jnp.concatenate([hi, lo], axis=0), preferred_element_type=F32) + carry)
            carry = carry + jnp.sum(lk, axis=0, keepdims=True)
        acc_l[...] = carry
        w = jnp.exp(log_beta + (jnp.concatenate(afters, axis=0) if len(afters) > 1 else afters[0]))
        if valid is not None:
            w = jnp.where(valid, w, 0.0)
        acc_o[...] += jnp.dot(w.T.astype(BF16), vall, preferred_element_type=F32)

    @pl.when(g == 0)
    def _():
        rows = heads * t_new
        qt = jnp.concatenate([q_ref[...]] * heads, axis=0)
        rh = _div_pow2(lax.broadcasted_iota(jnp.int32, (rows, SB_WIDTH), 0), t_new)
        ch = _div_pow2(lax.broadcasted_iota(jnp.int32, (rows, SB_WIDTH), 1), HEAD_DIM)
        qbd[...] = jnp.where(rh == ch, qt * (HEAD_DIM ** -0.5), 0.0).astype(BF16)
        acc_o[...] = jnp.zeros_like(acc_o)
        acc_l[...] = jnp.zeros_like(acc_l)
        pad = jnp.zeros((PAGE - t_new, SB_WIDTH), F32)
        kn = jnp.concatenate([kn_ref[...], pad], axis=0).astype(BF16)
        vn = jnp.concatenate([vn_ref[...], pad], axis=0).astype(BF16)
        s_idx = lax.broadcasted_iota(jnp.int32, (PAGE, rows), 0)
        t_idx = lax.broadcasted_iota(jnp.int32, (PAGE, rows), 1) & (t_new - 1)
        process([kn], [vn], s_idx < t_idx)

    def page(buf, idx):
        flat = buf.at[idx].reshape(PAGE * PITCH, HEAD_DIM)
        return jnp.concatenate([flat[pl.ds(h, PAGE, stride=PITCH), :] for h in range(heads)], axis=1).astype(BF16)

    for c in page_copies(n):
        c.wait()
    first = (n % PAGE_SETS) * npg
    process([page(kbuf, first + i) for i in range(npg)], [page(vbuf, first + i) for i in range(npg)], None)

    @pl.when(g == pl.num_programs(1) - 1)
    def _():
        for h in range(heads):
            o_ref[:, h * HEAD_DIM:(h + 1) * HEAD_DIM] = acc_o[h * t_new:(h + 1) * t_new,
                                                              h * HEAD_DIM:(h + 1) * HEAD_DIM]


def _sb_sample(q, k_new, v_new, b_sb, cache_k, cache_v, page_table, t_new, pages_per_step):
    batch, n_pages = page_table.shape
    assert n_pages % pages_per_step == 0 and t_new == SUBLANES and SB_HEADS * t_new == LANES
    steps = n_pages // pages_per_step
    j = np.arange(PAGE)
    tm = (j[None, :] > j[:, None]).astype(np.float32)
    tri = jnp.asarray(np.concatenate([tm, tm], axis=1), dtype=BF16)
    bias = jnp.repeat(b_sb.astype(F32), t_new)[None, :]
    row_spec = pl.BlockSpec((t_new, SB_WIDTH), lambda b, g, pt: (b, 0))
    assert batch * steps >= PAGE_SETS - 1
    n_bufs = PAGE_SETS * pages_per_step
    grid_spec = pltpu.PrefetchScalarGridSpec(
        num_scalar_prefetch=1,
        grid=(batch, steps),
        in_specs=[row_spec, row_spec, row_spec,
                  pl.BlockSpec((1, LANES), lambda b, g, pt: (0, 0)),
                  pl.BlockSpec((PAGE, 2 * PAGE), lambda b, g, pt: (0, 0)),
                  pl.BlockSpec(memory_space=pl.ANY),
                  pl.BlockSpec(memory_space=pl.ANY)],
        out_specs=row_spec,
        scratch_shapes=[pltpu.VMEM((LANES, SB_WIDTH), BF16),
                        pltpu.VMEM((LANES, SB_WIDTH), F32),
                        pltpu.VMEM((1, LANES), F32),
                        pltpu.VMEM((n_bufs, PAGE, PITCH, HEAD_DIM), F32),
                        pltpu.VMEM((n_bufs, PAGE, PITCH, HEAD_DIM), F32),
                        pltpu.SemaphoreType.DMA((2, n_bufs))],
    )
    return pl.pallas_call(
        functools.partial(_sb_sample_kernel, pages_per_step=pages_per_step, t_new=t_new, n_pages=n_pages),
        grid_spec=grid_spec,
        out_shape=jax.ShapeDtypeStruct(q.shape, F32),
        compiler_params=_params(("arbitrary", "arbitrary")),
        name="sb_sample",
    )(page_table, q, k_new, v_new, bias, tri, cache_k, cache_v)


def _mlstm_kernel(*refs, chunk, group, has_init):
    if has_init:
        pb_ref, g_ref, gb_ref, nw_ref, tril_ref, c0_ref, n0_ref, m0_ref = refs[:8]
        refs = refs[8:]
    else:
        pb_ref, g_ref, gb_ref, nw_ref, tril_ref = refs[:5]
        refs = refs[5:]
    hn_ref, c_out, n_out, m_out, c_s, n_s, m_s = refs
    L = chunk
    ci = pl.program_id(1)

    @pl.when(ci == 0)
    def _():
        if has_init:
            c_s[...] = c0_ref[...]
            n_s[...] = n0_ref[...]
            m_s[...] = m0_ref[...]
        else:
            c_s[...] = jnp.zeros_like(c_s)
            n_s[...] = jnp.zeros_like(n_s)
            m_s[...] = jnp.zeros_like(m_s)

    row = lax.broadcasted_iota(jnp.int32, (L, L), 0)
    col = lax.broadcasted_iota(jnp.int32, (L, L), 1)
    mask = col <= row
    g = g_ref[...] + gb_ref[...]
    lf = _log_sigmoid(g)
    if group:
        live = pl.program_id(0) & (L // group - 1)
        rlive = _div_pow2(lax.broadcasted_iota(jnp.int32, (L, 1), 0), group) == live
        mask = jnp.logical_and(mask, _div_pow2(col, group) == live)
        lf = jnp.where(rlive, lf, 0.0)
    hi, lo = _split_bf16(lf)
    bsum = jnp.dot(tril_ref[...], jnp.concatenate([hi, lo], axis=0), preferred_element_type=F32)
    g_t = g.T
    bsum_t = bsum.T

    for h in range(ML_HEADS):
        qf = pb_ref[:, h * ML_QK:(h + 1) * ML_QK]
        kf = pb_ref[:, ML_HEADS * ML_QK + h * ML_QK:ML_HEADS * ML_QK + (h + 1) * ML_QK] * (ML_QK ** -0.5)
        v0 = 2 * ML_HEADS * ML_QK
        vf = pb_ref[:, v0 + h * ML_V:v0 + (h + 1) * ML_V]
        ob = pb_ref[:, v0 + ML_WIDTH + h * ML_V:v0 + ML_WIDTH + (h + 1) * ML_V]
        q = qf.astype(BF16)
        k = kf.astype(BF16)
        b_col = bsum[:, ML_HEADS + h:ML_HEADS + h + 1]
        b_row = bsum_t[ML_HEADS + h:ML_HEADS + h + 1, :]
        i_col = g[:, h:h + 1]
        i_row = g_t[h:h + 1, :]
        m_prev = m_s[h:h + 1, 0:1]
        c_st = c_s[h]
        n_st = n_s[h]

        d = jnp.where(mask, b_col - b_row + i_row, -jnp.inf)
        inter = b_col + m_prev
        m_t = jnp.maximum(inter, jnp.max(d, axis=1, keepdims=True))
        w_inter = jnp.exp(inter - m_t)
        w_intra = jnp.exp(d - m_t) * lax.dot_general(q, k, (((1,), (1,)), ((), ())), preferred_element_type=F32)
        qc = lax.dot_general(q, c_st.astype(BF16), (((1,), (1,)), ((), ())), preferred_element_type=F32)
        num = w_inter * qc + jnp.dot(w_intra.astype(BF16), vf.astype(BF16), preferred_element_type=F32)
        den = w_inter * jnp.sum(qf * n_st, axis=1, keepdims=True) + jnp.sum(w_intra, axis=1, keepdims=True)
        hh = num / jnp.maximum(jnp.abs(den), jnp.exp(-m_t))

        b_last = b_col[L - 1:L, :]
        m_new = m_t[L - 1:L, :]
        g_state = jnp.exp(b_last + m_prev - m_new)
        g_rows = jnp.exp(b_last - b_col + i_col - m_new)
        if group:
            g_rows = jnp.where(rlive, g_rows, 0.0)
        gv_t = (g_rows * vf).T.astype(BF16)
        c_s[h] = g_state * c_st + jnp.dot(gv_t, k, preferred_element_type=F32)
        n_s[h] = g_state * n_st + jnp.sum(g_rows * kf, axis=0, keepdims=True)
        m_s[h:h + 1, :] = jnp.broadcast_to(m_new, (1, LANES))

        hn = hh * lax.rsqrt(jnp.mean(hh * hh, axis=1, keepdims=True) + EPS)
        hn = hn * nw_ref[:, h * ML_V:(h + 1) * ML_V] / (1.0 + jnp.exp(-ob))
        if group:
            hn_ref[:, h * ML_V:(h + 1) * ML_V] = _rows(hn, live * group, group).astype(hn_ref.dtype)
        else:
            hn_ref[:, h * ML_V:(h + 1) * ML_V] = hn.astype(hn_ref.dtype)

    @pl.when(ci == pl.num_programs(1) - 1)
    def _():
        c_out[...] = c_s[...]
        n_out[...] = n_s[...]
        m_out[...] = m_s[...]


def _rows(x, r0, n):
    nrows = x.shape[0]
    ridx = lax.broadcasted_iota(jnp.int32, (nrows, 1), 0)
    xm = jnp.where(jnp.logical_and(ridx >= r0, ridx < r0 + n), x, 0.0)
    out = xm[0:n]
    for s in range(1, nrows // n):
        out = out + xm[s * n:(s + 1) * n]
    return out


def _mlstm(pb, gates, gate_bias, norm_w, batch, seq, chunk, group, init, hn_dtype):
    rows, width = pb.shape
    L = chunk
    t = np.arange(L)
    tril = (t[None, :] <= t[:, None]).astype(np.float32)
    tril2 = jnp.asarray(np.concatenate([tril, tril], axis=1), dtype=BF16)
    if group:
        per = L // group
        nc = 1
        blk = lambda b, c: (b // per, 0)
        out_rows = group
        hn_blk = lambda b, c: (b, 0)
    else:
        nc = seq // L
        blk = lambda b, c: (b * nc + c, 0)
        out_rows = L
        hn_blk = blk
    const = lambda b, c: (0, 0)
    state4 = lambda b, c: (b, 0, 0, 0)
    state3 = lambda b, c: (b, 0, 0)
    in_specs = [pl.BlockSpec((L, width), blk),
                pl.BlockSpec((L, LANES), blk),
                pl.BlockSpec((1, LANES), const),
                pl.BlockSpec((1, ML_WIDTH), const),
                pl.BlockSpec((L, 2 * L), const)]
    args = [pb, gates, gate_bias, norm_w.reshape(1, ML_WIDTH), tril2]
    c_spec = pl.BlockSpec((None, ML_HEADS, ML_V, ML_QK), state4)
    n_spec = pl.BlockSpec((None, ML_HEADS, 1, ML_QK), state4)
    m_spec = pl.BlockSpec((None, ML_HEADS, LANES), state3)
    if init is not None:
        in_specs += [c_spec, n_spec, m_spec]
        args += list(init)
    return pl.pallas_call(
        functools.partial(_mlstm_kernel, chunk=L, group=group, has_init=init is not None),
        grid=(batch, nc),
        in_specs=in_specs,
        out_specs=[pl.BlockSpec((out_rows, ML_WIDTH), hn_blk), c_spec, n_spec, m_spec],
        out_shape=[jax.ShapeDtypeStruct((rows, ML_WIDTH), hn_dtype),
                   jax.ShapeDtypeStruct((batch, ML_HEADS, ML_V, ML_QK), F32),
                   jax.ShapeDtypeStruct((batch, ML_HEADS, 1, ML_QK), F32),
                   jax.ShapeDtypeStruct((batch, ML_HEADS, LANES), F32)],
        scratch_shapes=[pltpu.VMEM((ML_HEADS, ML_V, ML_QK), F32),
                        pltpu.VMEM((ML_HEADS, 1, ML_QK), F32),
                        pltpu.VMEM((ML_HEADS, LANES), F32)],
        compiler_params=_params(("arbitrary", "arbitrary")),
        name="mlstm",
    )(*args)


TM = 1024
TM_DOWN = 512
TN = 512
TN_FFN = 256
TM_NORM = 256


def kernel(x_prompt, x_sample, cache_k, cache_v, page_table, state_c, state_n, state_m, norm_mix_w, w_in, b_sb,
           b_igate, b_fgate, ml_norm_w, w_out, norm_ffn_w, w_gate, w_up, w_down, final_norm_w):
    depth = w_in.shape[0]
    assert depth == 1
    bp, sp, d = x_prompt.shape
    bs, ss, _ = x_sample.shape
    n_main = 3 * SB_WIDTH + 3 * ML_WIDTH
    l = 0
    w_gates = jnp.pad(w_in[l][:, n_main:], ((0, 0), (0, LANES - 2 * ML_HEADS)))
    gate_bias = jnp.concatenate([b_igate[l].astype(F32), b_fgate[l].astype(F32),
                                 jnp.zeros((LANES - 2 * ML_HEADS,), F32)])[None, :]
    xp = x_prompt.reshape(bp * sp, d)
    xs = x_sample.reshape(bs * ss, d)

    xn_p, xn_s = _rmsnorm(xp, xs, norm_mix_w[l], BF16, TM_NORM)
    pb_p, pb_s = _matmul(xn_p, xn_s, w_in[l], 3 * SB_WIDTH, 3 * ML_WIDTH, TM, TN, [F32], [F32], "inproj_ml")
    g_p, g_s = _matmul(xn_p, xn_s, w_gates, 0, LANES, TM, LANES, [F32], [F32], "inproj_gates")
    qt4, q_s = _matmul(xn_p, xn_s, w_in[l], 0, SB_WIDTH, TM, TN, [], [F32], "inproj_q",
                       tb=KV_BLK, scale=HEAD_DIM ** -0.5 * LOG2E)
    kp, k_b, ks = _matmul(xn_p, xn_s, w_in[l], SB_WIDTH, SB_WIDTH, TM, TN, [F32, BF16], [F32], "inproj_k")
    vp, vt4, vs = _matmul(xn_p, xn_s, w_in[l], 2 * SB_WIDTH, SB_WIDTH, TM, TN, [F32], [F32], "inproj_v", tb=KV_BLK)

    o_p = _sb_prompt(qt4, k_b, vt4, b_sb[l], bp, sp, 512, 4, BF16)
    hn_p, cp, np_, mp = _mlstm(pb_p, g_p, gate_bias, ml_norm_w[l], bp, sp, 128, 0, None, BF16)
    init = (state_c[l], state_n[l][:, :, None, :],
            jnp.broadcast_to(state_m[l][:, :, None], (bs, ML_HEADS, LANES)))
    o_s = _sb_sample(q_s, ks, vs, b_sb[l], cache_k[l], cache_v[l], page_table, ss, 4)
    hn_s, cs, ns, ms = _mlstm(pb_s, g_s, gate_bias, ml_norm_w[l], bs, ss, 128, ss, init, F32)

    x1_p, x1_s = _outproj(o_p, hn_p, xp, o_s, hn_s, xs, w_out[l], TM, TN)
    xn2_p, xn2_s = _rmsnorm(x1_p, x1_s, norm_ffn_w[l], BF16, TM_NORM)
    hid_p, hid_s = _gateup(xn2_p, xn2_s, w_gate[l], w_up[l], TM, TN_FFN)
    x2_p, x2_s = _down(hid_p, x1_p, hid_s, x1_s, w_down[l].astype(BF16), TM_DOWN, TN_FFN)
    y_prompt, y_sample = _rmsnorm(x2_p, x2_s, final_norm_w, F32, TM_NORM)
    y_prompt = y_prompt.reshape(bp, sp, d)
    y_sample = y_sample.reshape(bs, ss, d)
    return (y_prompt, y_sample,
            kp.reshape(1, bp, sp, SB_HEADS, HEAD_DIM), vp.reshape(1, bp, sp, SB_HEADS, HEAD_DIM),
            cp[None], np_[:, :, 0, :][None], mp[:, :, 0][None],
            ks.reshape(1, bs, ss, SB_HEADS, HEAD_DIM), vs.reshape(1, bs, ss, SB_HEADS, HEAD_DIM),
            cs[None], ns[:, :, 0, :][None], ms[:, :, 0][None])
```

```python
import functools
import math

import numpy as np
import jax
import jax.numpy as jnp
from jax import lax
from jax.experimental import pallas as pl
from jax.experimental.pallas import tpu as pltpu

F32 = jnp.float32
BF16 = jnp.bfloat16
EPS = 1e-6

HEAD_DIM = 128
SB_HEADS = 16
SB_WIDTH = SB_HEADS * HEAD_DIM
ML_HEADS = 4
ML_QK = 256
ML_V = 512
ML_WIDTH = ML_HEADS * ML_V
PAGE = 128
PITCH = 24
PAGE_SETS = 3
LANES = 128
SUBLANES = 8

VMEM_LIMIT = 56 * 1024 * 1024


def _params(sem):
    return pltpu.CompilerParams(dimension_semantics=sem, vmem_limit_bytes=VMEM_LIMIT)


def _log_sigmoid(x):
    return jnp.minimum(x, 0.0) - jnp.log(1.0 + jnp.exp(-jnp.abs(x)))


def _div_pow2(x, n):
    assert n & (n - 1) == 0
    return lax.shift_right_logical(x, int(math.log2(n)))


def _split_bf16(x):
    hi = x.astype(BF16)
    lo = (x - hi.astype(F32)).astype(BF16)
    return hi, lo


def _rmsnorm_kernel(xp_ref, xs_ref, w_ref, op_ref, os_ref):
    def norm(x):
        ms = jnp.mean(x * x, axis=-1, keepdims=True)
        return x * lax.rsqrt(ms + EPS) * w_ref[...]

    op_ref[...] = norm(xp_ref[...]).astype(op_ref.dtype)

    @pl.when(pl.program_id(0) == 0)
    def _():
        os_ref[...] = norm(xs_ref[...]).astype(os_ref.dtype)


def _rmsnorm(xp, xs, w, out_dtype, tm):
    mp, d = xp.shape
    ms = xs.shape[0]
    return pl.pallas_call(
        _rmsnorm_kernel,
        grid=(mp // tm,),
        in_specs=[pl.BlockSpec((tm, d), lambda i: (i, 0)),
                  pl.BlockSpec((ms, d), lambda i: (0, 0)),
                  pl.BlockSpec((1, d), lambda i: (0, 0))],
        out_specs=[pl.BlockSpec((tm, d), lambda i: (i, 0)),
                   pl.BlockSpec((ms, d), lambda i: (0, 0))],
        out_shape=[jax.ShapeDtypeStruct((mp, d), out_dtype), jax.ShapeDtypeStruct((ms, d), out_dtype)],
        compiler_params=_params(("arbitrary",)),
        name="rmsnorm",
    )(xp, xs, w.reshape(1, d))


def _sample_block(ms, tn, nj):
    return pl.BlockSpec((ms, tn), lambda i, j: (0, jnp.where(i == 0, j, nj - 1)))


def _mm_kernel(ap_ref, as_ref, w_ref, *o_refs, n_plain, tb, scale, w_rows):
    w = w_ref[...].astype(BF16)
    if w_rows:
        mm = lambda a: lax.dot_general(a, w, (((1,), (1,)), ((), ())), preferred_element_type=F32)
    else:
        mm = lambda a: jnp.dot(a, w, preferred_element_type=F32)
    acc = mm(ap_ref[...])
    for o_ref in o_refs[:n_plain]:
        o_ref[...] = acc.astype(o_ref.dtype)
    n_prompt = n_plain
    if tb:
        t_ref = o_refs[n_plain]
        n_prompt += 1
        tm, tn = acc.shape
        for blk in range(tm // tb):
            for hh in range(tn // HEAD_DIM):
                piece = acc[blk * tb:(blk + 1) * tb, hh * HEAD_DIM:(hh + 1) * HEAD_DIM] * scale
                t_ref[blk, hh] = piece.T.astype(t_ref.dtype)

    @pl.when(pl.program_id(0) == 0)
    def _():
        acc_s = mm(as_ref[...])
        for o_ref in o_refs[n_prompt:]:
            o_ref[...] = acc_s.astype(o_ref.dtype)


def _matmul(ap, as_, w, col_off, n_cols, tm, tn, p_dtypes, s_dtypes, name, tb=0, scale=1.0, w_rows=False):
    mp, k = ap.shape
    ms = as_.shape[0]
    assert col_off % tn == 0 and n_cols % tn == 0 and mp % tm == 0
    assert w.shape[1 if w_rows else 0] == k
    off = col_off // tn
    nj = n_cols // tn
    out_specs = [pl.BlockSpec((tm, tn), lambda i, j: (i, j)) for _ in p_dtypes]
    out_shape = [jax.ShapeDtypeStruct((mp, n_cols), dt) for dt in p_dtypes]
    if tb:
        assert tm % tb == 0 and tn % HEAD_DIM == 0
        out_specs.append(pl.BlockSpec((tm // tb, tn // HEAD_DIM, HEAD_DIM, tb), lambda i, j: (i, j, 0, 0)))
        out_shape.append(jax.ShapeDtypeStruct((mp // tb, n_cols // HEAD_DIM, HEAD_DIM, tb), BF16))
    out_specs += [_sample_block(ms, tn, nj) for _ in s_dtypes]
    out_shape += [jax.ShapeDtypeStruct((ms, n_cols), dt) for dt in s_dtypes]
    return pl.pallas_call(
        functools.partial(_mm_kernel, n_plain=len(p_dtypes), tb=tb, scale=scale, w_rows=w_rows),
        grid=(mp // tm, nj),
        in_specs=[pl.BlockSpec((tm, k), lambda i, j: (i, 0)),
                  pl.BlockSpec((ms, k), lambda i, j: (0, 0)),
                  pl.BlockSpec((tn, k), lambda i, j: (j + off, 0)) if w_rows
                  else pl.BlockSpec((k, tn), lambda i, j: (0, j + off))],
        out_specs=out_specs,
        out_shape=out_shape,
        compiler_params=_params(("arbitrary", "arbitrary")),
        name=name,
    )(ap, as_, w)


def _outproj_kernel(a1p_ref, a2p_ref, rp_ref, a1s_ref, a2s_ref, rs_ref, w1_ref, w2_ref, op_ref, os_ref):
    w1 = w1_ref[...].astype(BF16)
    w2 = w2_ref[...].astype(BF16)

    def proj(a1_ref, a2_ref, r_ref):
        acc = jnp.dot(a1_ref[...].astype(BF16), w1, preferred_element_type=F32)
        acc += jnp.dot(a2_ref[...].astype(BF16), w2, preferred_element_type=F32)
        return r_ref[...] + acc

    op_ref[...] = proj(a1p_ref, a2p_ref, rp_ref)

    @pl.when(pl.program_id(0) == 0)
    def _():
        os_ref[...] = proj(a1s_ref, a2s_ref, rs_ref)


def _outproj(a1p, a2p, resp, a1s, a2s, ress, w, tm, tn):
    mp, k1 = a1p.shape
    ms = a1s.shape[0]
    n = w.shape[1]
    nj = n // tn
    assert a2p.shape[1] == k1
    row = lambda i, j: (i, 0)
    whole = lambda i, j: (0, 0)
    return pl.pallas_call(
        _outproj_kernel,
        grid=(mp // tm, nj),
        in_specs=[pl.BlockSpec((tm, k1), row), pl.BlockSpec((tm, k1), row),
                  pl.BlockSpec((tm, tn), lambda i, j: (i, j)),
                  pl.BlockSpec((ms, k1), whole), pl.BlockSpec((ms, k1), whole),
                  _sample_block(ms, tn, nj),
                  pl.BlockSpec((k1, tn), lambda i, j: (0, j)),
                  pl.BlockSpec((k1, tn), lambda i, j: (1, j))],
        out_specs=[pl.BlockSpec((tm, tn), lambda i, j: (i, j)), _sample_block(ms, tn, nj)],
        out_shape=[jax.ShapeDtypeStruct((mp, n), F32), jax.ShapeDtypeStruct((ms, n), F32)],
        compiler_params=_params(("arbitrary", "arbitrary")),
        name="outproj",
    )(a1p, a2p, resp, a1s, a2s, ress, w, w)


def _gateup_kernel(ap_ref, as_ref, wg_ref, wu_ref, op_ref, os_ref):
    wg = wg_ref[...].astype(BF16)
    wu = wu_ref[...].astype(BF16)

    def swiglu(a):
        g = jnp.dot(a, wg, preferred_element_type=F32)
        u = jnp.dot(a, wu, preferred_element_type=F32)
        return g / (1.0 + jnp.exp(-g)) * u

    op_ref[...] = swiglu(ap_ref[...]).astype(op_ref.dtype)

    @pl.when(pl.program_id(0) == 0)
    def _():
        os_ref[...] = swiglu(as_ref[...]).astype(os_ref.dtype)


def _gateup(ap, as_, wg, wu, tm, tn):
    mp, k = ap.shape
    ms = as_.shape[0]
    f = wg.shape[1]
    assert f % tn == 0
    nj = f // tn
    return pl.pallas_call(
        _gateup_kernel,
        grid=(mp // tm, nj),
        in_specs=[pl.BlockSpec((tm, k), lambda i, j: (i, 0)),
                  pl.BlockSpec((ms, k), lambda i, j: (0, 0)),
                  pl.BlockSpec((k, tn), lambda i, j: (0, j)),
                  pl.BlockSpec((k, tn), lambda i, j: (0, j))],
        out_specs=[pl.BlockSpec((tm, tn), lambda i, j: (i, j)), _sample_block(ms, tn, nj)],
        out_shape=[jax.ShapeDtypeStruct((mp, f), BF16), jax.ShapeDtypeStruct((ms, f), BF16)],
        compiler_params=_params(("arbitrary", "arbitrary")),
        name="ffn_gateup",
    )(ap, as_, wg, wu)


def _down_kernel(ap_ref, rp_ref, as_ref, rs_ref, w_ref, op_ref, os_ref):
    w = w_ref[...]
    op_ref[...] = rp_ref[...] + jnp.dot(ap_ref[...], w, preferred_element_type=F32)

    @pl.when(pl.program_id(0) == 0)
    def _():
        os_ref[...] = rs_ref[...] + jnp.dot(as_ref[...], w, preferred_element_type=F32)


def _down(ap, resp, as_, ress, w, tm, tn):
    mp, k = ap.shape
    ms = as_.shape[0]
    n = w.shape[1]
    nj = n // tn
    return pl.pallas_call(
        _down_kernel,
        grid=(mp // tm, nj),
        in_specs=[pl.BlockSpec((tm, k), lambda i, j: (i, 0)),
                  pl.BlockSpec((tm, tn), lambda i, j: (i, j)),
                  pl.BlockSpec((ms, k), lambda i, j: (0, 0)),
                  _sample_block(ms, tn, nj),
                  pl.BlockSpec((k, tn), lambda i, j: (0, j))],
        out_specs=[pl.BlockSpec((tm, tn), lambda i, j: (i, j)), _sample_block(ms, tn, nj)],
        out_shape=[jax.ShapeDtypeStruct((mp, n), F32), jax.ShapeDtypeStruct((ms, n), F32)],
        compiler_params=_params(("arbitrary", "arbitrary")),
        name="ffn_down",
    )(ap, resp, as_, ress, w)


KV_BLK = 256
TOT_ROWS = 16
LOG2E = 1.4426950408889634


def _neg_abs(x):
    bits = pltpu.bitcast(x, jnp.uint32) | jnp.uint32(0x80000000)
    return pltpu.bitcast(bits, F32)


def _sb_prompt_kernel(qt_ref, k_ref, vt_ref, b_ref, tri_ref, o_ref, ot_acc, *, tq, hpg):
    qi = pl.program_id(2)
    nqb = tq // KV_BLK
    qts = [jnp.concatenate([qt_ref[i, h] for i in range(nqb)], axis=1) for h in range(hpg)]
    tri = tri_ref[...]
    ot_acc[...] = jnp.zeros_like(ot_acc)

    def blocks(base, carry, masked):
        js = [base + u for u in reversed(range(nqb))]
        if masked:
            qpos = qi * tq + lax.broadcasted_iota(jnp.int32, (KV_BLK, tq), 1)
            valids = [(j * KV_BLK + lax.broadcasted_iota(jnp.int32, (KV_BLK, tq), 0)) < qpos for j in js]
        log_betas, log_keeps = [], []
        for h in range(hpg):
            for n, j in enumerate(js):
                kb = k_ref[pl.ds(pl.multiple_of(j * KV_BLK, KV_BLK), KV_BLK), h * HEAD_DIM:(h + 1) * HEAD_DIM]
                z = jnp.dot(kb, qts[h], preferred_element_type=F32) + b_ref[h]
                l = jnp.log(1.0 + jnp.exp2(_neg_abs(z))) * LOG2E
                log_beta = jnp.minimum(z, 0.0) - l
                log_keep = log_beta - z
                if masked:
                    log_keep = jnp.where(valids[n], log_keep, 0.0)
                log_betas.append(log_beta)
                log_keeps.append(log_keep.astype(BF16))
        cums = [jnp.dot(tri, lk, preferred_element_type=F32) for lk in log_keeps]
        new_carry = []
        for h in range(hpg):
            c = carry[h]
            ws = []
            for n in range(nqb):
                i = h * nqb + n
                w = jnp.exp2(log_betas[i] + cums[i][:KV_BLK] + c)
                if masked:
                    w = jnp.where(valids[n], w, 0.0)
                ws.append(w.astype(BF16))
                c = c + cums[i][KV_BLK:KV_BLK + 1]
            new_carry.append(c)
            vt = jnp.concatenate([vt_ref[j, h] for j in js], axis=1)
            ot_acc[h] += jnp.dot(vt, jnp.concatenate(ws, axis=0), preferred_element_type=F32)
        return tuple(new_carry)

    zero = tuple(jnp.zeros((1, tq), F32) for _ in range(hpg))
    carry = blocks(qi * nqb, zero, True)
    lax.fori_loop(0, qi, lambda i, c: blocks((qi - 1 - i) * nqb, c, False), carry)
    for h in range(hpg):
        o_ref[:, h * HEAD_DIM:(h + 1) * HEAD_DIM] = ot_acc[h].T.astype(o_ref.dtype)


def _later_key_matrix():
    s = np.arange(KV_BLK)[:, None]
    j = np.arange(KV_BLK)[None, :]
    tri = np.concatenate([(j > s).astype(np.float32), np.ones((TOT_ROWS, KV_BLK), np.float32)], axis=0)
    return jnp.asarray(tri, dtype=BF16)


def _sb_prompt(qt4, k, vt4, b_sb, batch, seq, tq, hpg, out_dtype):
    heads = k.shape[1] // HEAD_DIM
    assert seq % tq == 0 and tq % KV_BLK == 0 and heads % hpg == 0
    nq = seq // tq
    nqb = tq // KV_BLK
    nkb = seq // KV_BLK
    b_rep = jnp.broadcast_to((b_sb.astype(F32) * LOG2E)[:, None, None], (heads, 1, tq))
    return pl.pallas_call(
        functools.partial(_sb_prompt_kernel, tq=tq, hpg=hpg),
        grid=(batch, heads // hpg, nq),
        in_specs=[pl.BlockSpec((nqb, hpg, HEAD_DIM, KV_BLK), lambda b, g, qi: (b * nq + qi, g, 0, 0)),
                  pl.BlockSpec((seq, hpg * HEAD_DIM), lambda b, g, qi: (b, g)),
                  pl.BlockSpec((nkb, hpg, HEAD_DIM, KV_BLK), lambda b, g, qi: (b, g, 0, 0)),
                  pl.BlockSpec((hpg, 1, tq), lambda b, g, qi: (g, 0, 0)),
                  pl.BlockSpec((KV_BLK + TOT_ROWS, KV_BLK), lambda b, g, qi: (0, 0))],
        out_specs=pl.BlockSpec((tq, hpg * HEAD_DIM), lambda b, g, qi: (b * nq + qi, g)),
        out_shape=jax.ShapeDtypeStruct(k.shape, out_dtype),
        scratch_shapes=[pltpu.VMEM((hpg, HEAD_DIM, tq), F32)],
        compiler_params=_params(("arbitrary", "arbitrary", "arbitrary")),
        name="sb_prompt",
    )(qt4, k, vt4, b_rep, _later_key_matrix())


def _sb_sample_kernel(pt_ref, q_ref, kn_ref, vn_ref, b_ref, tri_ref, ck_hbm, cv_hbm, o_ref,
                      qbd, acc_o, acc_l, kbuf, vbuf, sems, *, pages_per_step, t_new, n_pages):
    npg = pages_per_step
    b = pl.program_id(0)
    g = pl.program_id(1)
    steps = n_pages // npg
    n = b * steps + g
    total = pl.num_programs(0) * steps
    heads = SB_WIDTH // HEAD_DIM
    bias = b_ref[...]
    tri = tri_ref[...]

    def page_copies(m):
        bm = m // steps
        gm = m - bm * steps
        first = (m % PAGE_SETS) * npg
        copies = []
        for i in range(npg):
            page = pt_ref[bm, n_pages - 1 - (gm * npg + i)]
            for src, buf, kv in ((ck_hbm, kbuf, 0), (cv_hbm, vbuf, 1)):
                copies.append(pltpu.make_async_copy(src.at[page], buf.at[first + i, :, pl.ds(0, heads), :],
                                                    sems.at[kv, first + i]))
        return copies

    ahead = PAGE_SETS - 1

    @pl.when(n == 0)
    def _():
        for m in range(ahead):
            for c in page_copies(m):
                c.start()

    @pl.when(n + ahead < total)
    def _():
        for c in page_copies(n + ahead):
            c.start()

    def process(kps, vps, valid):
        kall = jnp.concatenate(kps, axis=0) if len(kps) > 1 else kps[0]
        vall = jnp.concatenate(vps, axis=0) if len(vps) > 1 else vps[0]
        z = lax.dot_general(kall, qbd[...], (((1,), (1,)), ((), ())), preferred_element_type=F32) + bias
        log_beta = _log_sigmoid(z)
        log_keep = log_beta - z
        if valid is not None:
            log_keep = jnp.where(valid, log_keep, 0.0)
        carry = acc_l[...]
        afters = []
        for i in range(len(kps)):
            lk = log_keep[i * PAGE:(i + 1) * PAGE]
            hi, lo = _split_bf16(lk)
            afters.append(jnp.dot(tri, jnp.concatenate([hi, lo], axis=0), preferred_element_type=F32) + carry)
            carry = carry + jnp.sum(lk, axis=0, keepdims=True)
        acc_l[...] = carry
        w = jnp.exp(log_beta + (jnp.concatenate(afters, axis=0) if len(afters) > 1 else afters[0]))
        if valid is not None:
            w = jnp.where(valid, w, 0.0)
        acc_o[...] += jnp.dot(w.T.astype(BF16), vall, preferred_element_type=F32)

    @pl.when(g == 0)
    def _():
        rows = heads * t_new
        qt = jnp.concatenate([q_ref[...]] * heads, axis=0)
        rh = _div_pow2(lax.broadcasted_iota(jnp.int32, (rows, SB_WIDTH), 0), t_new)
        ch = _div_pow2(lax.broadcasted_iota(jnp.int32, (rows, SB_WIDTH), 1), HEAD_DIM)
        qbd[...] = jnp.where(rh == ch, qt * (HEAD_DIM ** -0.5), 0.0).astype(BF16)
        acc_o[...] = jnp.zeros_like(acc_o)
        acc_l[...] = jnp.zeros_like(acc_l)
        pad = jnp.zeros((PAGE - t_new, SB_WIDTH), F32)
        kn = jnp.concatenate([kn_ref[...], pad], axis=0).astype(BF16)
        vn = jnp.concatenate([vn_ref[...], pad], axis=0).astype(BF16)
        s_idx = lax.broadcasted_iota(jnp.int32, (PAGE, rows), 0)
        t_idx = lax.broadcasted_iota(jnp.int32, (PAGE, rows), 1) & (t_new - 1)
        process([kn], [vn], s_idx < t_idx)

    def page(buf, idx):
        flat = buf.at[idx].reshape(PAGE * PITCH, HEAD_DIM)
        return jnp.concatenate([flat[pl.ds(h, PAGE, stride=PITCH), :] for h in range(heads)], axis=1).astype(BF16)

    for c in page_copies(n):
        c.wait()
    first = (n % PAGE_SETS) * npg
    process([page(kbuf, first + i) for i in range(npg)], [page(vbuf, first + i) for i in range(npg)], None)

    @pl.when(g == pl.num_programs(1) - 1)
    def _():
        for h in range(heads):
            o_ref[:, h * HEAD_DIM:(h + 1) * HEAD_DIM] = acc_o[h * t_new:(h + 1) * t_new,
                                                              h * HEAD_DIM:(h + 1) * HEAD_DIM]


def _sb_sample(q, k_new, v_new, b_sb, cache_k, cache_v, page_table, t_new, pages_per_step):
    batch, n_pages = page_table.shape
    assert n_pages % pages_per_step == 0 and t_new == SUBLANES and SB_HEADS * t_new == LANES
    steps = n_pages // pages_per_step
    j = np.arange(PAGE)
    tm = (j[None, :] > j[:, None]).astype(np.float32)
    tri = jnp.asarray(np.concatenate([tm, tm], axis=1), dtype=BF16)
    bias = jnp.repeat(b_sb.astype(F32), t_new)[None, :]
    row_spec = pl.BlockSpec((t_new, SB_WIDTH), lambda b, g, pt: (b, 0))
    assert batch * steps >= PAGE_SETS - 1
    n_bufs = PAGE_SETS * pages_per_step
    grid_spec = pltpu.PrefetchScalarGridSpec(
        num_scalar_prefetch=1,
        grid=(batch, steps),
        in_specs=[row_spec, row_spec, row_spec,
                  pl.BlockSpec((1, LANES), lambda b, g, pt: (0, 0)),
                  pl.BlockSpec((PAGE, 2 * PAGE), lambda b, g, pt: (0, 0)),
                  pl.BlockSpec(memory_space=pl.ANY),
                  pl.BlockSpec(memory_space=pl.ANY)],
        out_specs=row_spec,
        scratch_shapes=[pltpu.VMEM((LANES, SB_WIDTH), BF16),
                        pltpu.VMEM((LANES, SB_WIDTH), F32),
                        pltpu.VMEM((1, LANES), F32),
                        pltpu.VMEM((n_bufs, PAGE, PITCH, HEAD_DIM), F32),
                        pltpu.VMEM((n_bufs, PAGE, PITCH, HEAD_DIM), F32),
                        pltpu.SemaphoreType.DMA((2, n_bufs))],
    )
    return pl.pallas_call(
        functools.partial(_sb_sample_kernel, pages_per_step=pages_per_step, t_new=t_new, n_pages=n_pages),
        grid_spec=grid_spec,
        out_shape=jax.ShapeDtypeStruct(q.shape, F32),
        compiler_params=_params(("arbitrary", "arbitrary")),
        name="sb_sample",
    )(page_table, q, k_new, v_new, bias, tri, cache_k, cache_v)


def _mlstm_kernel(*refs, chunk, group, has_init):
    if has_init:
        pb_ref, g_ref, gb_ref, nw_ref, tril_ref, c0_ref, n0_ref, m0_ref = refs[:8]
        refs = refs[8:]
    else:
        pb_ref, g_ref, gb_ref, nw_ref, tril_ref = refs[:5]
        refs = refs[5:]
    hn_ref, c_out, n_out, m_out, c_s, n_s, m_s = refs
    L = chunk
    ci = pl.program_id(1)

    @pl.when(ci == 0)
    def _():
        if has_init:
            c_s[...] = c0_ref[...]
            n_s[...] = n0_ref[...]
            m_s[...] = m0_ref[...]
        else:
            c_s[...] = jnp.zeros_like(c_s)
            n_s[...] = jnp.zeros_like(n_s)
            m_s[...] = jnp.zeros_like(m_s)

    row = lax.broadcasted_iota(jnp.int32, (L, L), 0)
    col = lax.broadcasted_iota(jnp.int32, (L, L), 1)
    mask = col <= row
    g = g_ref[...] + gb_ref[...]
    lf = _log_sigmoid(g)
    if group:
        live = pl.program_id(0) & (L // group - 1)
        rlive = _div_pow2(lax.broadcasted_iota(jnp.int32, (L, 1), 0), group) == live
        mask = jnp.logical_and(mask, _div_pow2(col, group) == live)
        lf = jnp.where(rlive, lf, 0.0)
    hi, lo = _split_bf16(lf)
    bsum = jnp.dot(tril_ref[...], jnp.concatenate([hi, lo], axis=0), preferred_element_type=F32)
    g_t = g.T
    bsum_t = bsum.T

    for h in range(ML_HEADS):
        qf = pb_ref[:, h * ML_QK:(h + 1) * ML_QK]
        kf = pb_ref[:, ML_HEADS * ML_QK + h * ML_QK:ML_HEADS * ML_QK + (h + 1) * ML_QK] * (ML_QK ** -0.5)
        v0 = 2 * ML_HEADS * ML_QK
        vf = pb_ref[:, v0 + h * ML_V:v0 + (h + 1) * ML_V]
        ob = pb_ref[:, v0 + ML_WIDTH + h * ML_V:v0 + ML_WIDTH + (h + 1) * ML_V]
        q = qf.astype(BF16)
        k = kf.astype(BF16)
        b_col = bsum[:, ML_HEADS + h:ML_HEADS + h + 1]
        b_row = bsum_t[ML_HEADS + h:ML_HEADS + h + 1, :]
        i_col = g[:, h:h + 1]
        i_row = g_t[h:h + 1, :]
        m_prev = m_s[h:h + 1, 0:1]
        c_st = c_s[h]
        n_st = n_s[h]

        d = jnp.where(mask, b_col - b_row + i_row, -jnp.inf)
        inter = b_col + m_prev
        m_t = jnp.maximum(inter, jnp.max(d, axis=1, keepdims=True))
        w_inter = jnp.exp(inter - m_t)
        w_intra = jnp.exp(d - m_t) * lax.dot_general(q, k, (((1,), (1,)), ((), ())), preferred_element_type=F32)
        qc = lax.dot_general(q, c_st.astype(BF16), (((1,), (1,)), ((), ())), preferred_element_type=F32)
        num = w_inter * qc + jnp.dot(w_intra.astype(BF16), vf.astype(BF16), preferred_element_type=F32)
        den = w_inter * jnp.sum(qf * n_st, axis=1, keepdims=True) + jnp.sum(w_intra, axis=1, keepdims=True)
        hh = num / jnp.maximum(jnp.abs(den), jnp.exp(-m_t))

        b_last = b_col[L - 1:L, :]
        m_new = m_t[L - 1:L, :]
        g_state = jnp.exp(b_last + m_prev - m_new)
        g_rows = jnp.exp(b_last - b_col + i_col - m_new)
        if group:
            g_rows = jnp.where(rlive, g_rows, 0.0)
        gv_t = (g_rows * vf).T.astype(BF16)
        c_s[h] = g_state * c_st + jnp.dot(gv_t, k, preferred_element_type=F32)
        n_s[h] = g_state * n_st + jnp.sum(g_rows * kf, axis=0, keepdims=True)
        m_s[h:h + 1, :] = jnp.broadcast_to(m_new, (1, LANES))

        hn = hh * lax.rsqrt(jnp.mean(hh * hh, axis=1, keepdims=True) + EPS)
        hn = hn * nw_ref[:, h * ML_V:(h + 1) * ML_V] / (1.0 + jnp.exp(-ob))
        if group:
            hn_ref[:, h * ML_V:(h + 1) * ML_V] = _rows(hn, live * group, group).astype(hn_ref.dtype)
        else:
            hn_ref[:, h * ML_V:(h + 1) * ML_V] = hn.astype(hn_ref.dtype)

    @pl.when(ci == pl.num_programs(1) - 1)
    def _():
        c_out[...] = c_s[...]
        n_out[...] = n_s[...]
        m_out[...] = m_s[...]


def _rows(x, r0, n):
    nrows = x.shape[0]
    ridx = lax.broadcasted_iota(jnp.int32, (nrows, 1), 0)
    xm = jnp.where(jnp.logical_and(ridx >= r0, ridx < r0 + n), x, 0.0)
    out = xm[0:n]
    for s in range(1, nrows // n):
        out = out + xm[s * n:(s + 1) * n]
    return out


def _mlstm(pb, gates, gate_bias, norm_w, batch, seq, chunk, group, init, hn_dtype):
    rows, width = pb.shape
    L = chunk
    t = np.arange(L)
    tril = (t[None, :] <= t[:, None]).astype(np.float32)
    tril2 = jnp.asarray(np.concatenate([tril, tril], axis=1), dtype=BF16)
    if group:
        per = L // group
        nc = 1
        blk = lambda b, c: (b // per, 0)
        out_rows = group
        hn_blk = lambda b, c: (b, 0)
    else:
        nc = seq // L
        blk = lambda b, c: (b * nc + c, 0)
        out_rows = L
        hn_blk = blk
    const = lambda b, c: (0, 0)
    state4 = lambda b, c: (b, 0, 0, 0)
    state3 = lambda b, c: (b, 0, 0)
    in_specs = [pl.BlockSpec((L, width), blk),
                pl.BlockSpec((L, LANES), blk),
                pl.BlockSpec((1, LANES), const),
                pl.BlockSpec((1, ML_WIDTH), const),
                pl.BlockSpec((L, 2 * L), const)]
    args = [pb, gates, gate_bias, norm_w.reshape(1, ML_WIDTH), tril2]
    c_spec = pl.BlockSpec((None, ML_HEADS, ML_V, ML_QK), state4)
    n_spec = pl.BlockSpec((None, ML_HEADS, 1, ML_QK), state4)
    m_spec = pl.BlockSpec((None, ML_HEADS, LANES), state3)
    if init is not None:
        in_specs += [c_spec, n_spec, m_spec]
        args += list(init)
    return pl.pallas_call(
        functools.partial(_mlstm_kernel, chunk=L, group=group, has_init=init is not None),
        grid=(batch, nc),
        in_specs=in_specs,
        out_specs=[pl.BlockSpec((out_rows, ML_WIDTH), hn_blk), c_spec, n_spec, m_spec],
        out_shape=[jax.ShapeDtypeStruct((rows, ML_WIDTH), hn_dtype),
                   jax.ShapeDtypeStruct((batch, ML_HEADS, ML_V, ML_QK), F32),
                   jax.ShapeDtypeStruct((batch, ML_HEADS, 1, ML_QK), F32),
                   jax.ShapeDtypeStruct((batch, ML_HEADS, LANES), F32)],
        scratch_shapes=[pltpu.VMEM((ML_HEADS, ML_V, ML_QK), F32),
                        pltpu.VMEM((ML_HEADS, 1, ML_QK), F32),
                        pltpu.VMEM((ML_HEADS, LANES), F32)],
        compiler_params=_params(("arbitrary", "arbitrary")),
        name="mlstm",
    )(*args)


TM = 1024
TM_DOWN = 512
TN = 512
TN_FFN = 256
TM_NORM = 256


def kernel(x_prompt, x_sample, cache_k, cache_v, page_table, state_c, state_n, state_m, norm_mix_w, w_in, b_sb,
           b_igate, b_fgate, ml_norm_w, w_out, norm_ffn_w, w_gate, w_up, w_down, final_norm_w):
    depth = w_in.shape[0]
    assert depth == 1
    bp, sp, d = x_prompt.shape
    bs, ss, _ = x_sample.shape
    n_main = 3 * SB_WIDTH + 3 * ML_WIDTH
    l = 0
    gate_bias = jnp.concatenate([b_igate[l].astype(F32), b_fgate[l].astype(F32),
                                 jnp.zeros((LANES - 2 * ML_HEADS,), F32)])[None, :]
    xp = x_prompt.reshape(bp * sp, d)
    xs = x_sample.reshape(bs * ss, d)

    xn_p, xn_s = _rmsnorm(xp, xs, norm_mix_w[l], BF16, TM_NORM)
    w_in_t = jnp.swapaxes(w_in[l], 0, 1)
    w_gates_t = jnp.pad(w_in_t[n_main:], ((0, LANES - 2 * ML_HEADS), (0, 0)))
    pb_p, pb_s = _matmul(xn_p, xn_s, w_in_t, 3 * SB_WIDTH, 3 * ML_WIDTH, TM, TN, [F32], [F32], "inproj_ml",
                         w_rows=True)
    g_p, g_s = _matmul(xn_p, xn_s, w_gates_t, 0, LANES, TM, LANES, [F32], [F32], "inproj_gates", w_rows=True)
    qt4, q_s = _matmul(xn_p, xn_s, w_in_t, 0, SB_WIDTH, TM, TN, [], [F32], "inproj_q",
                       tb=KV_BLK, scale=HEAD_DIM ** -0.5 * LOG2E, w_rows=True)
    kp, k_b, ks = _matmul(xn_p, xn_s, w_in_t, SB_WIDTH, SB_WIDTH, TM, TN, [F32, BF16], [F32], "inproj_k",
                          w_rows=True)
    vp, vt4, vs = _matmul(xn_p, xn_s, w_in_t, 2 * SB_WIDTH, SB_WIDTH, TM, TN, [F32], [F32], "inproj_v",
                          tb=KV_BLK, w_rows=True)

    o_p = _sb_prompt(qt4, k_b, vt4, b_sb[l], bp, sp, 512, 4, BF16)
    hn_p, cp, np_, mp = _mlstm(pb_p, g_p, gate_bias, ml_norm_w[l], bp, sp, 128, 0, None, BF16)
    init = (state_c[l], state_n[l][:, :, None, :],
            jnp.broadcast_to(state_m[l][:, :, None], (bs, ML_HEADS, LANES)))
    o_s = _sb_sample(q_s, ks, vs, b_sb[l], cache_k[l], cache_v[l], page_table, ss, 4)
    hn_s, cs, ns, ms = _mlstm(pb_s, g_s, gate_bias, ml_norm_w[l], bs, ss, 128, ss, init, F32)

    x1_p, x1_s = _outproj(o_p, hn_p, xp, o_s, hn_s, xs, w_out[l], TM, TN)
    xn2_p, xn2_s = _rmsnorm(x1_p, x1_s, norm_ffn_w[l], BF16, TM_NORM)
    hid_p, hid_s = _gateup(xn2_p, xn2_s, w_gate[l], w_up[l], TM, TN_FFN)
    x2_p, x2_s = _down(hid_p, x1_p, hid_s, x1_s, w_down[l].astype(BF16), TM_DOWN, TN_FFN)
    y_prompt, y_sample = _rmsnorm(x2_p, x2_s, final_norm_w, F32, TM_NORM)
    y_prompt = y_prompt.reshape(bp, sp, d)
    y_sample = y_sample.reshape(bs, ss, d)
    return (y_prompt, y_sample,
            kp.reshape(1, bp, sp, SB_HEADS, HEAD_DIM), vp.reshape(1, bp, sp, SB_HEADS, HEAD_DIM),
            cp[None], np_[:, :, 0, :][None], mp[:, :, 0][None],
            ks.reshape(1, bs, ss, SB_HEADS, HEAD_DIM), vs.reshape(1, bs, ss, SB_HEADS, HEAD_DIM),
            cs[None], ns[:, :, 0, :][None], ms[:, :, 0][None])
```
